```python
import jax
import jax.numpy as jnp
from jax import lax
import numpy as np

D_MODEL = 2048
BATCH = 2
SEQ = 8192
DEPTH = 2

GRID_W = 64
CTX_LEN = 256
Q_BLOCK = 128
ROPE_THETA = 10000.0
NORM_EPS = 1e-6

MLA_HEADS = 16
MLA_Q_RANK = 512
MLA_KV_RANK = 512
MLA_NOPE_DIM = 128
MLA_ROPE_DIM = 64
MLA_V_DIM = 128

GQA_HEADS = 16
GQA_KV_HEADS = 4
GQA_GROUP = GQA_HEADS // GQA_KV_HEADS
GQA_HEAD_DIM = 128

RNN_WIDTH = D_MODEL
RNN_BLOCKS = 8
RNN_BLOCK_W = RNN_WIDTH // RNN_BLOCKS
CONV_WIDTH = 4
LRU_C = 8.0
N_DIRECTIONS = 2

N_BRANCHES = 3
FFN_HIDDEN = -(-8 * D_MODEL // (3 * 256)) * 256

IN_SPLITS = (
    MLA_KV_RANK,
    MLA_ROPE_DIM,
    GQA_KV_HEADS * GQA_HEAD_DIM,
    GQA_KV_HEADS * GQA_HEAD_DIM,
    RNN_WIDTH,
    MLA_Q_RANK,
    GQA_HEADS * GQA_HEAD_DIM,
    RNN_WIDTH,
    N_BRANCHES * D_MODEL,
)
CTX_SIDE_COLS = MLA_KV_RANK + MLA_ROPE_DIM + 2 * GQA_KV_HEADS * GQA_HEAD_DIM + RNN_WIDTH
IN_COLS = CTX_SIDE_COLS + MLA_Q_RANK + GQA_HEADS * GQA_HEAD_DIM + RNN_WIDTH + N_BRANCHES * D_MODEL

kernel_name = "hybrid_mla_gqa_rglru_prefix_dit_block"


def rms_norm(x, w):
    xf = x.astype(jnp.float32)
    y = xf * lax.rsqrt(jnp.mean(jnp.square(xf), axis=-1, keepdims=True) + NORM_EPS)
    return (y * w.astype(jnp.float32)).astype(x.dtype)


def modulate(x, shift, scale):
    return x * (1 + scale) + shift


def split_cols(z, sizes):
    out, start = [], 0
    for s in sizes:
        out.append(z[..., start:start + s])
        start += s
    return out


def rope_half(x, ang):
    x1, x2 = jnp.split(x, 2, axis=-1)
    cos, sin = jnp.cos(ang), jnp.sin(ang)
    return jnp.concatenate([x1 * cos - x2 * sin, x2 * cos + x1 * sin], axis=-1)


def axial_rope(x, row, col):
    d = x.shape[-1]
    quarter = d // 4
    inv = ROPE_THETA ** (-jnp.arange(quarter, dtype=jnp.float32) / quarter)
    shape = (row.shape[0],) + (1,) * (x.ndim - 3) + (quarter,)
    ang_r = (row[:, None] * inv).reshape(shape)
    ang_c = (col[:, None] * inv).reshape(shape)
    xr, xc = jnp.split(x.astype(jnp.float32), 2, axis=-1)
    return jnp.concatenate([rope_half(xr, ang_r), rope_half(xc, ang_c)], axis=-1).astype(x.dtype)


def sweep_query_blocks(fn, qs):
    b, n = qs[0].shape[:2]
    nb = n // Q_BLOCK
    blocks = tuple(jnp.swapaxes(q.reshape((b, nb, Q_BLOCK) + q.shape[2:]), 0, 1) for q in qs)
    out = lax.map(fn, blocks)
    return jnp.swapaxes(out, 0, 1).reshape((b, n) + out.shape[3:])


def mla_keys(ckv, k_rope, lp, row, col):
    b, n, _ = ckv.shape
    kv = (rms_norm(ckv, lp["mla_kv_norm_w"]) @ lp["mla_kv_up"]).reshape(b, n, MLA_HEADS, MLA_NOPE_DIM + MLA_V_DIM)
    k_nope, v = jnp.split(kv, [MLA_NOPE_DIM], axis=-1)
    if row is not None:
        k_rope = axial_rope(k_rope, row, col)
    return k_nope, k_rope, v


def mla_queries(cq, lp, row, col):
    b, n, _ = cq.shape
    q = (rms_norm(cq, lp["mla_q_norm_w"]) @ lp["mla_q_up"]).reshape(b, n, MLA_HEADS, MLA_NOPE_DIM + MLA_ROPE_DIM)
    q_nope, q_rope = jnp.split(q, [MLA_NOPE_DIM], axis=-1)
    if row is not None:
        q_rope = axial_rope(q_rope, row, col)
    return q_nope, q_rope


def mla_attend(k_nope, k_rope, v):
    scale = (MLA_NOPE_DIM + MLA_ROPE_DIM) ** -0.5

    def fn(qb):
        q_nope, q_rope = qb
        s = (jnp.einsum("bqhd,blhd->bhql", q_nope, k_nope, preferred_element_type=jnp.float32)
             + jnp.einsum("bqhd,bld->bhql", q_rope, k_rope, preferred_element_type=jnp.float32)) * scale
        p = jax.nn.softmax(s, axis=-1).astype(v.dtype)
        return jnp.einsum("bhql,blhd->bqhd", p, v)
    return fn


def gqa_keys(gk, gv, lp, row, col):
    b, n, _ = gk.shape
    k = rms_norm(gk.reshape(b, n, GQA_KV_HEADS, GQA_HEAD_DIM), lp["gqa_k_norm_w"])
    v = gv.reshape(b, n, GQA_KV_HEADS, GQA_HEAD_DIM)
    if row is not None:
        k = axial_rope(k, row, col)
    return k, v


def gqa_queries(gq, lp, row, col):
    b, n, _ = gq.shape
    q = rms_norm(gq.reshape(b, n, GQA_KV_HEADS, GQA_GROUP, GQA_HEAD_DIM), lp["gqa_q_norm_w"])
    if row is not None:
        q = axial_rope(q, row, col)
    return q


def gqa_attend(k, v):
    scale = GQA_HEAD_DIM ** -0.5

    def fn(qb):
        (q,) = qb
        s = jnp.einsum("bqngd,blnd->bngql", q, k, preferred_element_type=jnp.float32) * scale
        p = jax.nn.softmax(s, axis=-1).astype(v.dtype)
        return jnp.einsum("bngql,blnd->bqngd", p, v)
    return fn


def short_conv(x, w, bias):
    y = lax.conv_general_dilated(
        x, w[:, None, :].astype(x.dtype), window_strides=(1,),
        padding=[(CONV_WIDTH // 2, CONV_WIDTH - 1 - CONV_WIDTH // 2)],
        dimension_numbers=("NWC", "WIO", "NWC"), feature_group_count=x.shape[-1])
    return y + bias


def block_diag(x, w, bias):
    b, n, _ = x.shape
    y = jnp.einsum("bnkc,kcd->bnkd", x.reshape(b, n, RNN_BLOCKS, RNN_BLOCK_W), w)
    return y.reshape(b, n, RNN_WIDTH) + bias


def lru_coeffs(xc, w_a, b_a, w_x, b_x, lam):
    r = jax.nn.sigmoid(block_diag(xc, w_a, b_a).astype(jnp.float32))
    i = jax.nn.sigmoid(block_diag(xc, w_x, b_x).astype(jnp.float32))
    log_a = -LRU_C * r * jax.nn.softplus(-lam.astype(jnp.float32))
    mult = jnp.sqrt(jnp.maximum(-jnp.expm1(2.0 * log_a), 0.0))
    return jnp.exp(log_a), mult * i * xc.astype(jnp.float32)


def linear_scan(a, b, h0, reverse):
    def combine(e1, e2):
        a1, b1 = e1
        a2, b2 = e2
        return a1 * a2, a2 * b1 + b2
    a_cum, b_cum = lax.associative_scan(combine, (a, b), reverse=reverse, axis=1)
    return a_cum * h0[:, None, :] + b_cum


def rglru_scans(xc, lp, h0_f, h0_b):
    a_f, b_f = lru_coeffs(xc, lp["rg_a_w"][0], lp["rg_a_b"][0], lp["rg_x_w"][0], lp["rg_x_b"][0], lp["rg_lambda"][0])
    a_b, b_b = lru_coeffs(xc, lp["rg_a_w"][1], lp["rg_a_b"][1], lp["rg_x_w"][1], lp["rg_x_b"][1], lp["rg_lambda"][1])
    return linear_scan(a_f, b_f, h0_f, False), linear_scan(a_b, b_b, h0_b, True)


def merge_branches(o_mla, o_gqa, o_rg, gate_logits, lp):
    b, n = o_mla.shape[:2]
    g_mla, g_gqa, g_rg = jnp.split(jax.nn.sigmoid(gate_logits), N_BRANCHES, axis=-1)
    y = (g_mla * (o_mla.reshape(b, n, -1) @ lp["mla_out"])
         + g_gqa * (o_gqa.reshape(b, n, -1) @ lp["gqa_out"])
         + g_rg * (o_rg @ lp["rg_out"]))
    return y @ lp["merge_out"]


def mixing_sublayer(u_lat, u_ctx, row, col, lp, ctx_out):
    dt = u_lat.dtype
    b = u_lat.shape[0]
    z_lat = u_lat @ lp["w_in"]
    if ctx_out:
        ctx_parts = split_cols(u_ctx @ lp["w_in"], IN_SPLITS)
    else:
        ctx_parts = split_cols(u_ctx @ lp["w_in"][:, :CTX_SIDE_COLS], IN_SPLITS[:5])
    ckv_l, kr_l, gk_l, gv_l, rx_l, cq_l, gq_l, ry_l, gate_l = split_cols(z_lat, IN_SPLITS)
    ckv_c, kr_c, gk_c, gv_c, rx_c = ctx_parts[:5]

    mk_nope_c, mk_rope_c, mv_c = mla_keys(ckv_c, kr_c, lp, None, None)
    mk_nope_l, mk_rope_l, mv_l = mla_keys(ckv_l, kr_l, lp, row, col)
    mq_nope_l, mq_rope_l = mla_queries(cq_l, lp, row, col)
    mla_l = sweep_query_blocks(
        mla_attend(jnp.concatenate([mk_nope_c, mk_nope_l], axis=1),
                   jnp.concatenate([mk_rope_c, mk_rope_l], axis=1),
                   jnp.concatenate([mv_c, mv_l], axis=1)),
        (mq_nope_l, mq_rope_l))

    gk_c_, gv_c_ = gqa_keys(gk_c, gv_c, lp, None, None)
    gk_l_, gv_l_ = gqa_keys(gk_l, gv_l, lp, row, col)
    gq_l_ = gqa_queries(gq_l, lp, row, col)
    gqa_l = sweep_query_blocks(
        gqa_attend(jnp.concatenate([gk_c_, gk_l_], axis=1), jnp.concatenate([gv_c_, gv_l_], axis=1)),
        (gq_l_,))

    h_zero = jnp.zeros((b, RNN_WIDTH), jnp.float32)
    hf_c, hb_c = rglru_scans(short_conv(rx_c, lp["rg_conv_w"], lp["rg_conv_b"]), lp, h_zero, h_zero)
    hf_l, hb_l = rglru_scans(short_conv(rx_l, lp["rg_conv_w"], lp["rg_conv_b"]), lp, hf_c[:, -1], hb_c[:, 0])
    rg_l = (hf_l + hb_l).astype(dt) * jax.nn.gelu(ry_l)

    y_lat = merge_branches(mla_l, gqa_l, rg_l, gate_l, lp)
    if not ctx_out:
        return y_lat, None

    cq_c, gq_c, ry_c, gate_c = ctx_parts[5:]
    mq_nope_c, mq_rope_c = mla_queries(cq_c, lp, None, None)
    mla_c = mla_attend(mk_nope_c, mk_rope_c, mv_c)((mq_nope_c, mq_rope_c))
    gqa_c = gqa_attend(gk_c_, gv_c_)((gqa_queries(gq_c, lp, None, None),))
    rg_c = (hf_c + hb_c).astype(dt) * jax.nn.gelu(ry_c)
    y_ctx = merge_branches(mla_c, gqa_c, rg_c, gate_c, lp)
    return y_lat, y_ctx


def swiglu(u, w_gate, w_up, w_down):
    return (jax.nn.silu(u @ w_gate) * (u @ w_up)) @ w_down


def setup_inputs(seed: int = 0) -> dict:
    key = jax.random.key(seed)
    ks = iter(jax.random.split(key, 40))
    f32 = jnp.float32
    L, D = DEPTH, D_MODEL

    def nrm(shape, fan_in, gain=1.0):
        return (gain * fan_in ** -0.5) * jax.random.normal(next(ks), shape, f32)

    def gain_vec(shape):
        return 1.0 + 0.05 * jax.random.normal(next(ks), shape, f32)

    def small(shape):
        return 0.02 * jax.random.normal(next(ks), shape, f32)

    x = jax.random.normal(next(ks), (BATCH, SEQ, D), f32)
    c = jax.random.normal(next(ks), (BATCH, D), f32)
    ctx = jax.random.normal(next(ks), (BATCH, CTX_LEN, D), f32)
    c_ctx = jax.random.normal(next(ks), (D,), f32)
    mod_w = nrm((L, D, 6 * D), D, 0.5)
    mod_b = small((L, 6 * D))
    norm_mix_w = gain_vec((L, D))
    norm_ffn_w = gain_vec((L, D))
    w_in = nrm((L, D, IN_COLS), D)
    mla_q_norm_w = gain_vec((L, MLA_Q_RANK))
    mla_q_up = nrm((L, MLA_Q_RANK, MLA_HEADS * (MLA_NOPE_DIM + MLA_ROPE_DIM)), MLA_Q_RANK)
    mla_kv_norm_w = gain_vec((L, MLA_KV_RANK))
    mla_kv_up = nrm((L, MLA_KV_RANK, MLA_HEADS * (MLA_NOPE_DIM + MLA_V_DIM)), MLA_KV_RANK)
    mla_out = nrm((L, MLA_HEADS * MLA_V_DIM, D), MLA_HEADS * MLA_V_DIM)
    gqa_q_norm_w = gain_vec((L, GQA_HEAD_DIM))
    gqa_k_norm_w = gain_vec((L, GQA_HEAD_DIM))
    gqa_out = nrm((L, GQA_HEADS * GQA_HEAD_DIM, D), GQA_HEADS * GQA_HEAD_DIM)
    rg_conv_w = nrm((L, CONV_WIDTH, RNN_WIDTH), CONV_WIDTH)
    rg_conv_b = small((L, RNN_WIDTH))
    rg_a_w = nrm((L, N_DIRECTIONS, RNN_BLOCKS, RNN_BLOCK_W, RNN_BLOCK_W), RNN_BLOCK_W)
    rg_a_b = small((L, N_DIRECTIONS, RNN_WIDTH))
    rg_x_w = nrm((L, N_DIRECTIONS, RNN_BLOCKS, RNN_BLOCK_W, RNN_BLOCK_W), RNN_BLOCK_W)
    rg_x_b = small((L, N_DIRECTIONS, RNN_WIDTH))
    u = jax.random.uniform(next(ks), (L, N_DIRECTIONS, RNN_WIDTH), f32, 0.9, 0.999)
    p = u ** (1.0 / LRU_C)
    rg_lambda = jnp.log(p) - jnp.log1p(-p)
    rg_out = nrm((L, RNN_WIDTH, D), RNN_WIDTH)
    merge_out = nrm((L, D, D), D)
    ffn_w_gate = nrm((L, D, FFN_HIDDEN), D)
    ffn_w_up = nrm((L, D, FFN_HIDDEN), D)
    ffn_w_down = nrm((L, FFN_HIDDEN, D), FFN_HIDDEN)
    final_norm_w = gain_vec((D,))
    return {
        "x": x, "c": c, "ctx": ctx, "c_ctx": c_ctx,
        "mod_w": mod_w, "mod_b": mod_b, "norm_mix_w": norm_mix_w, "norm_ffn_w": norm_ffn_w,
        "w_in": w_in,
        "mla_q_norm_w": mla_q_norm_w, "mla_q_up": mla_q_up, "mla_kv_norm_w": mla_kv_norm_w,
        "mla_kv_up": mla_kv_up, "mla_out": mla_out,
        "gqa_q_norm_w": gqa_q_norm_w, "gqa_k_norm_w": gqa_k_norm_w, "gqa_out": gqa_out,
        "rg_conv_w": rg_conv_w, "rg_conv_b": rg_conv_b, "rg_a_w": rg_a_w, "rg_a_b": rg_a_b,
        "rg_x_w": rg_x_w, "rg_x_b": rg_x_b, "rg_lambda": rg_lambda, "rg_out": rg_out,
        "merge_out": merge_out,
        "ffn_w_gate": ffn_w_gate, "ffn_w_up": ffn_w_up, "ffn_w_down": ffn_w_down,
        "final_norm_w": final_norm_w,
    }


def reference(x, c, ctx, c_ctx, mod_w, mod_b, norm_mix_w, norm_ffn_w, w_in,
              mla_q_norm_w, mla_q_up, mla_kv_norm_w, mla_kv_up, mla_out,
              gqa_q_norm_w, gqa_k_norm_w, gqa_out,
              rg_conv_w, rg_conv_b, rg_a_w, rg_a_b, rg_x_w, rg_x_b, rg_lambda, rg_out,
              merge_out, ffn_w_gate, ffn_w_up, ffn_w_down, final_norm_w):
    seq = x.shape[1]
    rows = seq // GRID_W
    row = jnp.repeat(jnp.arange(rows, dtype=jnp.float32), GRID_W)
    col = jnp.tile(jnp.arange(GRID_W, dtype=jnp.float32), rows)

    h_lat, h_ctx = x, ctx
    for l in range(DEPTH):
        ctx_out = l < DEPTH - 1
        lp = {
            "w_in": w_in[l],
            "mla_q_norm_w": mla_q_norm_w[l], "mla_q_up": mla_q_up[l],
            "mla_kv_norm_w": mla_kv_norm_w[l], "mla_kv_up": mla_kv_up[l], "mla_out": mla_out[l],
            "gqa_q_norm_w": gqa_q_norm_w[l], "gqa_k_norm_w": gqa_k_norm_w[l], "gqa_out": gqa_out[l],
            "rg_conv_w": rg_conv_w[l], "rg_conv_b": rg_conv_b[l],
            "rg_a_w": rg_a_w[l], "rg_a_b": rg_a_b[l], "rg_x_w": rg_x_w[l], "rg_x_b": rg_x_b[l],
            "rg_lambda": rg_lambda[l], "rg_out": rg_out[l], "merge_out": merge_out[l],
        }
        m_lat = jnp.split((jax.nn.silu(c) @ mod_w[l] + mod_b[l])[:, None, :], 6, axis=-1)
        m_ctx = jnp.split(jax.nn.silu(c_ctx) @ mod_w[l] + mod_b[l], 6, axis=-1)

        u_lat = modulate(rms_norm(h_lat, norm_mix_w[l]), m_lat[0], m_lat[1])
        u_ctx = modulate(rms_norm(h_ctx, norm_mix_w[l]), m_ctx[0], m_ctx[1])
        y_lat, y_ctx = mixing_sublayer(u_lat, u_ctx, row, col, lp, ctx_out)

        h_lat = h_lat + m_lat[2] * y_lat
        h_lat = h_lat + m_lat[5] * swiglu(modulate(rms_norm(h_lat, norm_ffn_w[l]), m_lat[3], m_lat[4]),
                                          ffn_w_gate[l], ffn_w_up[l], ffn_w_down[l])
        if ctx_out:
            h_ctx = h_ctx + m_ctx[2] * y_ctx
            h_ctx = h_ctx + m_ctx[5] * swiglu(modulate(rms_norm(h_ctx, norm_ffn_w[l]), m_ctx[3], m_ctx[4]),
                                              ffn_w_gate[l], ffn_w_up[l], ffn_w_down[l])
    return rms_norm(h_lat, final_norm_w)
```

```python
import functools
import math

import jax
import jax.numpy as jnp
from jax import lax
from jax.experimental import pallas as pl
from jax.experimental.pallas import tpu as pltpu

F32 = jnp.float32
BF16 = jnp.bfloat16

D_MODEL = 2048
GRID_W = 64
ROPE_THETA = 10000.0
NORM_EPS = 1e-6

MLA_HEADS = 16
MLA_RANK = 512
MLA_NOPE = 128
MLA_ROPE = 64
MLA_V = 128
MLA_QK_PAD = 256

GQA_HEADS = 16
GQA_KV = 4
GQA_GROUP = GQA_HEADS // GQA_KV
GQA_HD = 128

RNN_BLOCKS = 8
RNN_BW = D_MODEL // RNN_BLOCKS
CONV_W = 4
LRU_C = 8.0
FFN_HIDDEN = 5632

C_CKV = 0
C_KR = 512
C_GK = 640
C_GV = 1152
C_RX = 2048
C_GQ = 4096
C_RY = 6144
C_GATE = 8192
C_CQ = 14336
Z_COLS = 14848

LANE = 128
VMEM_CAP = 64 * 1024 * 1024
VMEM_LIMIT = 56 * 1024 * 1024

ROW_TILE = 768
ROW_CHUNK = 128
PREP_TILE = 384
Q_TILE = 256
KV_CHUNK = 512
SCAN_TILE = 256
LOG2E = 1.4426950408889634


def _params(sem):
    return pltpu.CompilerParams(dimension_semantics=sem, vmem_limit_bytes=VMEM_LIMIT)


def _dot(a, b):
    return jnp.dot(a, b, preferred_element_type=F32)


def _rms(x, w):
    return x * lax.rsqrt(jnp.mean(x * x, axis=-1, keepdims=True) + NORM_EPS) * w


def _is_ctx_rows(tile_idx, tiles_per_batch, tm, s_lat, r0=0, n=None):
    n = tm if n is None else n
    r = (tile_idx % tiles_per_batch) * tm + r0 + lax.broadcasted_iota(jnp.int32, (n, 1), 0)
    return r >= s_lat


def _mod_norm_to(u_scr, h_ref, nw_ref, ml_ref, mc_ref, tile_idx, tpb, tm, s_lat, k_shift, k_scale):
    nw = nw_ref[...]
    for r0 in range(0, tm, ROW_CHUNK):
        is_ctx = _is_ctx_rows(tile_idx, tpb, tm, s_lat, r0, ROW_CHUNK)
        y = _rms(h_ref[r0:r0 + ROW_CHUNK, :], nw)
        shift = jnp.where(is_ctx, mc_ref[0, k_shift:k_shift + 1], ml_ref[0, k_shift:k_shift + 1])
        scale = jnp.where(is_ctx, mc_ref[0, k_scale:k_scale + 1], ml_ref[0, k_scale:k_scale + 1])
        u_scr[r0:r0 + ROW_CHUNK, :] = (y * (1.0 + scale) + shift).astype(BF16)


def _rope(x, cos, sin_signed, half):
    n = x.shape[-1]
    lane = lax.broadcasted_iota(jnp.int32, x.shape, 1)
    up = pltpu.roll(x, n - half, axis=1)
    dn = pltpu.roll(x, half, axis=1)
    partner = jnp.where((lane % (2 * half)) < half, up, dn)
    return x * cos + partner * sin_signed


def _mods_kernel(c_ref, w_ref, b_ref, o_ref):
    x = c_ref[...]
    x = x * jax.nn.sigmoid(x)
    w = w_ref[0]
    xh = x.astype(BF16)
    xl = (x - xh.astype(F32)).astype(BF16)
    wh = w.astype(BF16)
    wl = (w - wh.astype(F32)).astype(BF16)
    o_ref[0] = _dot(xh, wh) + _dot(xl, wh) + _dot(xh, wl) + b_ref[0]


def _mods(cs, mod_w, mod_b):
    depth, d, n = mod_w.shape
    tn = 512
    return pl.pallas_call(
        _mods_kernel,
        grid=(depth, n // tn),
        in_specs=[
            pl.BlockSpec((8, d), lambda l, j: (0, 0)),
            pl.BlockSpec((1, d, tn), lambda l, j: (l, 0, j)),
            pl.BlockSpec((1, 1, tn), lambda l, j: (l, 0, j)),
        ],
        out_specs=pl.BlockSpec((1, 8, tn), lambda l, j: (l, 0, j)),
        out_shape=jax.ShapeDtypeStruct((depth, 8, n), F32),
        compiler_params=_params(("arbitrary", "arbitrary")),
        name="adaln_mods",
    )(cs, mod_w, mod_b.reshape(depth, 1, n))


def _inproj_kernel(h_ref, nw_ref, ml_ref, mc_ref, w_ref, z_ref, u_scr, *, tm, tpb, s_lat):
    i = pl.program_id(0)

    @pl.when(pl.program_id(1) == 0)
    def _():
        _mod_norm_to(u_scr, h_ref, nw_ref, ml_ref, mc_ref, i, tpb, tm, s_lat, 0, 1)

    z_ref[...] = _dot(u_scr[...], w_ref[...]).astype(z_ref.dtype)


def _inproj(h, nw, mods, w_pad, *, s_lat, s_tot):
    r, d = h.shape
    tm, tn = ROW_TILE, 512
    tpb = s_tot // tm
    return pl.pallas_call(
        functools.partial(_inproj_kernel, tm=tm, tpb=tpb, s_lat=s_lat),
        grid=(r // tm, Z_COLS // tn),
        in_specs=[
            pl.BlockSpec((tm, d), lambda i, j: (i, 0)),
            pl.BlockSpec((1, d), lambda i, j: (0, 0)),
            pl.BlockSpec((1, 6, d), lambda i, j: (i // tpb, 0, 0)),
            pl.BlockSpec((1, 6, d), lambda i, j: (2, 0, 0)),
            pl.BlockSpec((d, tn), lambda i, j: (0, j)),
        ],
        out_specs=pl.BlockSpec((tm, tn), lambda i, j: (i, j)),
        out_shape=jax.ShapeDtypeStruct((r, Z_COLS), BF16),
        scratch_shapes=[pltpu.VMEM((tm, d), BF16)],
        compiler_params=_params(("arbitrary", "arbitrary")),
        name="in_proj",
    )(h, nw, mods, mods, w_pad)


def _prep_kernel(za_ref, cq_ref, gq_ref, kvw_ref, qw_ref, kvup_ref, qup_ref, gkw_ref, gqw_ref,
                 cm_ref, sm_ref, cg_ref, sg_ref,
                 kmla_ref, vtm_ref, qtm_ref, kg_ref, vtg_ref, qtg_ref):
    cos_m, sin_m = cm_ref[...], sm_ref[...]
    cos_g, sin_g = cg_ref[...], sg_ref[...]
    mla_scale = (MLA_NOPE + MLA_ROPE) ** -0.5 * LOG2E
    gqa_scale = GQA_HD ** -0.5 * LOG2E

    ckv = _rms(za_ref[:, C_CKV:C_CKV + MLA_RANK].astype(F32), kvw_ref[...]).astype(BF16)
    kr = _rope(za_ref[:, C_KR:C_KR + LANE].astype(F32), cos_m, sin_m, MLA_ROPE // 4).astype(BF16)
    for h in range(MLA_HEADS):
        kv = _dot(ckv, kvup_ref[:, h * 256:(h + 1) * 256])
        kmla_ref[h, :, 0:MLA_NOPE] = kv[:, 0:MLA_NOPE].astype(BF16)
        kmla_ref[h, :, MLA_NOPE:MLA_QK_PAD] = kr
        vtm_ref[h] = kv[:, MLA_NOPE:].T.astype(BF16)

    cq = _rms(cq_ref[...].astype(F32), qw_ref[...]).astype(BF16)
    for h in range(MLA_HEADS):
        q = _dot(cq, qup_ref[:, h * MLA_QK_PAD:(h + 1) * MLA_QK_PAD])
        q_rope = _rope(q[:, MLA_NOPE:], cos_m, sin_m, MLA_ROPE // 4)
        qtm_ref[h, 0:MLA_NOPE, :] = (q[:, 0:MLA_NOPE] * mla_scale).T.astype(BF16)
        qtm_ref[h, MLA_NOPE:MLA_QK_PAD, :] = (q_rope * mla_scale).T.astype(BF16)

    for n in range(GQA_KV):
        k = _rms(za_ref[:, C_GK + n * GQA_HD:C_GK + (n + 1) * GQA_HD].astype(F32), gkw_ref[...])
        kg_ref[n] = _rope(k, cos_g, sin_g, GQA_HD // 4).astype(BF16)
        vtg_ref[n] = za_ref[:, C_GV + n * GQA_HD:C_GV + (n + 1) * GQA_HD].astype(F32).T.astype(BF16)
    for h in range(GQA_HEADS):
        q = _rms(gq_ref[:, h * GQA_HD:(h + 1) * GQA_HD].astype(F32), gqw_ref[...])
        q = _rope(q, cos_g, sin_g, GQA_HD // 4) * gqa_scale
        qtg_ref[h] = q.T.astype(BF16)


def _prep(z, lp, tabs, *, s_tot):
    r = z.shape[0]
    tm = PREP_TILE
    tpb = s_tot // tm
    cos_m, sin_m, cos_g, sin_g = tabs
    row = lambda i: (i, 0)
    tab = lambda i: (i % tpb, 0)
    const = lambda i: (0, 0)
    outs = [
        jax.ShapeDtypeStruct((MLA_HEADS, r, MLA_QK_PAD), BF16),
        jax.ShapeDtypeStruct((MLA_HEADS, MLA_V, r), BF16),
        jax.ShapeDtypeStruct((MLA_HEADS, MLA_QK_PAD, r), BF16),
        jax.ShapeDtypeStruct((GQA_KV, r, GQA_HD), BF16),
        jax.ShapeDtypeStruct((GQA_KV, GQA_HD, r), BF16),
        jax.ShapeDtypeStruct((GQA_HEADS, GQA_HD, r), BF16),
    ]
    return pl.pallas_call(
        _prep_kernel,
        grid=(r // tm,),
        in_specs=[
            pl.BlockSpec((tm, 2048), row),
            pl.BlockSpec((tm, MLA_RANK), lambda i: (i, C_CQ // MLA_RANK)),
            pl.BlockSpec((tm, 2048), lambda i: (i, C_GQ // 2048)),
            pl.BlockSpec((1, MLA_RANK), const),
            pl.BlockSpec((1, MLA_RANK), const),
            pl.BlockSpec((MLA_RANK, MLA_HEADS * 256), const),
            pl.BlockSpec((MLA_RANK, MLA_HEADS * MLA_QK_PAD), const),
            pl.BlockSpec((1, GQA_HD), const),
            pl.BlockSpec((1, GQA_HD), const),
            pl.BlockSpec((tm, LANE), tab),
            pl.BlockSpec((tm, LANE), tab),
            pl.BlockSpec((tm, LANE), tab),
            pl.BlockSpec((tm, LANE), tab),
        ],
        out_specs=[
            pl.BlockSpec((MLA_HEADS, tm, MLA_QK_PAD), lambda i: (0, i, 0)),
            pl.BlockSpec((MLA_HEADS, MLA_V, tm), lambda i: (0, 0, i)),
            pl.BlockSpec((MLA_HEADS, MLA_QK_PAD, tm), lambda i: (0, 0, i)),
            pl.BlockSpec((GQA_KV, tm, GQA_HD), lambda i: (0, i, 0)),
            pl.BlockSpec((GQA_KV, GQA_HD, tm), lambda i: (0, 0, i)),
            pl.BlockSpec((GQA_HEADS, GQA_HD, tm), lambda i: (0, 0, i)),
        ],
        out_shape=outs,
        compiler_params=_params(("arbitrary",)),
        name="attn_prep",
    )(z, z, z, lp["mla_kv_norm_w"], lp["mla_q_norm_w"], lp["mla_kv_up"], lp["mla_q_up"],
      lp["gqa_k_norm_w"], lp["gqa_q_norm_w"], cos_m, sin_m, cos_g, sin_g)


def _attn_kernel(qt_ref, k_ref, vt_ref, o_ref, acc_scr, m_scr, l_scr, *, heads, tq, s_lat, n_ctx, ctx_tile):
    qi = pl.program_id(2)
    qcat = jnp.concatenate([qt_ref[g] for g in range(heads)], axis=1) if heads > 1 else qt_ref[0]
    m_scr[...] = jnp.full(m_scr.shape, -1e30, F32)
    l_scr[...] = jnp.zeros(l_scr.shape, F32)
    acc_scr[...] = jnp.zeros(acc_scr.shape, F32)

    def step(start, size):
        k = k_ref[0, pl.ds(start, size), :]
        vt = vt_ref[0, :, pl.ds(start, size)]
        s = _dot(k, qcat)
        m_old = m_scr[...]
        m_new = jnp.maximum(m_old, jnp.max(s, axis=0, keepdims=True))
        alpha = jnp.exp2(m_old - m_new)
        p = jnp.exp2(s - m_new)
        l_scr[...] = alpha * l_scr[...] + jnp.sum(p, axis=0, keepdims=True)
        acc_scr[...] = alpha * acc_scr[...] + _dot(vt, p.astype(BF16))
        m_scr[...] = m_new

    n_lat = jnp.where(qi == ctx_tile, 0, s_lat // KV_CHUNK)

    def body(c, carry):
        step(pl.multiple_of(c * KV_CHUNK, KV_CHUNK), KV_CHUNK)
        return carry

    lax.fori_loop(0, n_lat, body, 0)
    step(s_lat, n_ctx)

    inv_l = 1.0 / l_scr[...]
    for g in range(heads):
        o = acc_scr[:, g * tq:(g + 1) * tq] * inv_l[:, g * tq:(g + 1) * tq]
        o_ref[:, g * GQA_HD:(g + 1) * GQA_HD] = o.T.astype(o_ref.dtype)


def _attention(qt, k, vt, *, heads, s_lat, s_tot, batch, name):
    n_heads, dk, r = qt.shape
    n_kv = k.shape[0]
    tq = Q_TILE
    n_ctx = s_tot - s_lat
    qpb = s_tot // tq
    ctx_tile = s_lat // tq
    n_q = qpb
    kern = functools.partial(_attn_kernel, heads=heads, tq=tq, s_lat=s_lat, n_ctx=n_ctx, ctx_tile=ctx_tile)
    return pl.pallas_call(
        kern,
        grid=(batch, n_kv, n_q),
        in_specs=[
            pl.BlockSpec((heads, dk, tq), lambda b, n, q: (n, 0, b * qpb + q)),
            pl.BlockSpec((1, s_tot, dk), lambda b, n, q: (n, b, 0)),
            pl.BlockSpec((1, GQA_HD, s_tot), lambda b, n, q: (n, 0, b)),
        ],
        out_specs=pl.BlockSpec((tq, heads * GQA_HD), lambda b, n, q: (b * qpb + q, n)),
        out_shape=jax.ShapeDtypeStruct((r, n_heads * GQA_HD), BF16),
        scratch_shapes=[
            pltpu.VMEM((GQA_HD, heads * tq), F32),
            pltpu.VMEM((1, heads * tq), F32),
            pltpu.VMEM((1, heads * tq), F32),
        ],
        compiler_params=_params(("arbitrary", "arbitrary", "arbitrary")),
        name=name,
    )(qt, k, vt)


def _shift_rows(x, s, fill, reverse):
    n = x.shape[0]
    t = lax.broadcasted_iota(jnp.int32, x.shape, 0)
    if reverse:
        return jnp.where(t < n - s, pltpu.roll(x, n - s, axis=0), fill)
    return jnp.where(t >= s, pltpu.roll(x, s, axis=0), fill)


def _tile_scan(a, b, h0, reverse):
    s = 1
    while s < a.shape[0]:
        b = a * _shift_rows(b, s, 0.0, reverse) + b
        a = a * _shift_rows(a, s, 1.0, reverse)
        s *= 2
    return a * h0 + b


def _gelu_tanh(x):
    return 0.5 * x * (1.0 + jnp.tanh(math.sqrt(2.0 / math.pi) * (x + 0.044715 * x * x * x)))


def _rglru_kernel(rx_ref, ry_ref, cw_ref, cb_ref, wa_ref, ba_ref, wx_ref, bx_ref, lam_ref,
                  o_ref, xp_scr, hf_scr, *, s_lat, n_ctx):
    tt = SCAN_TILE
    pad = 8
    lat0 = pad
    ctx0 = pad + s_lat + pad
    zeros = jnp.zeros((pad, RNN_BW), F32)
    xp_scr[0:pad, :] = zeros
    xp_scr[lat0 + s_lat:ctx0, :] = zeros
    xp_scr[ctx0 + n_ctx:ctx0 + n_ctx + pad, :] = zeros
    xp_scr[lat0:lat0 + s_lat, :] = rx_ref[0:s_lat, :].astype(F32)
    xp_scr[ctx0:ctx0 + n_ctx, :] = rx_ref[s_lat:s_lat + n_ctx, :].astype(F32)

    cw = cw_ref[...]
    cb = cb_ref[...]
    lam = -lam_ref[...]
    softplus = jnp.log1p(jnp.exp(-jnp.abs(lam))) + jnp.maximum(lam, 0.0)

    def coeffs(xp0, n, d):
        ext = xp_scr[pl.ds(xp0 - pad, n + 2 * pad), :]
        xc = cb
        for j in range(CONV_W):
            o = pad - CONV_W // 2 + j
            xc = xc + ext[o:o + n, :] * cw[j:j + 1, :]
        xb = xc.astype(BF16)
        r = jax.nn.sigmoid(_dot(xb, wa_ref[d, 0]) + ba_ref[d:d + 1, :])
        i = jax.nn.sigmoid(_dot(xb, wx_ref[d, 0]) + bx_ref[d:d + 1, :])
        log_a = -LRU_C * r * softplus[d:d + 1, :]
        a = jnp.exp(log_a)
        mult = jnp.sqrt(jnp.maximum(jnp.tanh(-log_a) * (1.0 + a * a), 0.0))
        return a, mult * i * xc

    def scan_tile(row0, xp0, n, d, h0):
        a, b = coeffs(xp0, n, d)
        h = _tile_scan(a, b, h0, reverse=(d == 1))
        if d == 0:
            hf_scr[pl.ds(row0, n), :] = h
            return h[n - 1:n, :]
        y = ry_ref[pl.ds(row0, n), :].astype(F32)
        o_ref[pl.ds(row0, n), :] = ((hf_scr[pl.ds(row0, n), :] + h) * _gelu_tanh(y)).astype(o_ref.dtype)
        return h[0:1, :]

    n_lat_tiles = s_lat // tt
    h0 = jnp.zeros((1, RNN_BW), F32)

    hc = scan_tile(s_lat, ctx0, n_ctx, 0, h0)

    def fwd(t, h):
        r0 = pl.multiple_of(t * tt, tt)
        return scan_tile(r0, r0 + lat0, tt, 0, h)

    lax.fori_loop(0, n_lat_tiles, fwd, hc)

    hc = scan_tile(s_lat, ctx0, n_ctx, 1, h0)

    def bwd(t, h):
        r0 = pl.multiple_of((n_lat_tiles - 1 - t) * tt, tt)
        return scan_tile(r0, r0 + lat0, tt, 1, h)

    lax.fori_loop(0, n_lat_tiles, bwd, hc)


def _rglru(z, lp, *, s_lat, s_tot, batch):
    r = z.shape[0]
    n_ctx = s_tot - s_lat
    bw = RNN_BW
    kern = functools.partial(_rglru_kernel, s_lat=s_lat, n_ctx=n_ctx)
    vec = lambda b, k: (0, k)
    return pl.pallas_call(
        kern,
        grid=(batch, RNN_BLOCKS),
        in_specs=[
            pl.BlockSpec((s_tot, bw), lambda b, k: (b, C_RX // bw + k)),
            pl.BlockSpec((s_tot, bw), lambda b, k: (b, C_RY // bw + k)),
            pl.BlockSpec((CONV_W, bw), vec),
            pl.BlockSpec((1, bw), vec),
            pl.BlockSpec((2, 1, bw, bw), lambda b, k: (0, k, 0, 0)),
            pl.BlockSpec((2, bw), vec),
            pl.BlockSpec((2, 1, bw, bw), lambda b, k: (0, k, 0, 0)),
            pl.BlockSpec((2, bw), vec),
            pl.BlockSpec((2, bw), vec),
        ],
        out_specs=pl.BlockSpec((s_tot, bw), lambda b, k: (b, k)),
        out_shape=jax.ShapeDtypeStruct((r, D_MODEL), BF16),
        scratch_shapes=[
            pltpu.VMEM((s_tot + 24, bw), F32),
            pltpu.VMEM((s_tot, bw), F32),
        ],
        compiler_params=_params(("arbitrary", "arbitrary")),
        name="rglru",
    )(z, z, lp["rg_conv_w"], lp["rg_conv_b"], lp["rg_a_w"], lp["rg_a_b"], lp["rg_x_w"], lp["rg_x_b"],
      lp["rg_lambda"])


def _merge_kernel(om_ref, og_ref, rg_ref, gm_ref, gg_ref, gr_ref, wm_ref, wg_ref, wr_ref, y_ref):
    y = (jax.nn.sigmoid(gm_ref[...].astype(F32)) * _dot(om_ref[...], wm_ref[...])
         + jax.nn.sigmoid(gg_ref[...].astype(F32)) * _dot(og_ref[...], wg_ref[...])
         + jax.nn.sigmoid(gr_ref[...].astype(F32)) * _dot(rg_ref[...], wr_ref[...]))
    y_ref[...] = y.astype(y_ref.dtype)


def _merge(o_mla, o_gqa, rg, z, lp):
    r, d = o_mla.shape
    tm, tn = ROW_TILE, 512
    gate0 = C_GATE // tn
    per = d // tn
    act = pl.BlockSpec((tm, d), lambda i, j: (i, 0))
    wgt = pl.BlockSpec((d, tn), lambda i, j: (0, j))
    return pl.pallas_call(
        _merge_kernel,
        grid=(r // tm, d // tn),
        in_specs=[
            act, act, act,
            pl.BlockSpec((tm, tn), lambda i, j: (i, gate0 + j)),
            pl.BlockSpec((tm, tn), lambda i, j: (i, gate0 + per + j)),
            pl.BlockSpec((tm, tn), lambda i, j: (i, gate0 + 2 * per + j)),
            wgt, wgt, wgt,
        ],
        out_specs=pl.BlockSpec((tm, tn), lambda i, j: (i, j)),
        out_shape=jax.ShapeDtypeStruct((r, d), BF16),
        compiler_params=_params(("arbitrary", "arbitrary")),
        name="merge_branches",
    )(o_mla, o_gqa, rg, z, z, z, lp["mla_out"], lp["gqa_out"], lp["rg_out"])


def _resid_kernel(y_ref, w_ref, h_ref, ml_ref, mc_ref, o_ref, *, tm, tpb, s_lat, tn):
    i, j = pl.program_id(0), pl.program_id(1)
    is_ctx = _is_ctx_rows(i, tpb, tm, s_lat)
    col = pl.ds(pl.multiple_of(j * tn, tn), tn)
    gate = jnp.where(is_ctx, mc_ref[0, 2:3, col], ml_ref[0, 2:3, col])
    o_ref[...] = h_ref[...] + gate * _dot(y_ref[...], w_ref[...])


def _merge_out(y, w, h, mods, *, s_lat, s_tot):
    r, d = h.shape
    tm, tn = ROW_TILE, 512
    tpb = s_tot // tm
    return pl.pallas_call(
        functools.partial(_resid_kernel, tm=tm, tpb=tpb, s_lat=s_lat, tn=tn),
        grid=(r // tm, d // tn),
        in_specs=[
            pl.BlockSpec((tm, d), lambda i, j: (i, 0)),
            pl.BlockSpec((d, tn), lambda i, j: (0, j)),
            pl.BlockSpec((tm, tn), lambda i, j: (i, j)),
            pl.BlockSpec((1, 6, d), lambda i, j: (i // tpb, 0, 0)),
            pl.BlockSpec((1, 6, d), lambda i, j: (2, 0, 0)),
        ],
        out_specs=pl.BlockSpec((tm, tn), lambda i, j: (i, j)),
        out_shape=jax.ShapeDtypeStruct((r, d), F32),
        compiler_params=_params(("arbitrary", "arbitrary")),
        name="merge_out_residual",
    )(y, w, h, mods, mods)


def _ffn_kernel(h_ref, nw_ref, ml_ref, mc_ref, wg_ref, wu_ref, wd_ref, fw_ref, o_ref, u_scr,
                *, tm, tpb, s_lat, final_norm):
    i, j = pl.program_id(0), pl.program_id(1)

    @pl.when(j == 0)
    def _():
        _mod_norm_to(u_scr, h_ref, nw_ref, ml_ref, mc_ref, i, tpb, tm, s_lat, 3, 4)

    u = u_scr[...]
    g = _dot(u, wg_ref[...])
    g = g * jax.nn.sigmoid(g) * _dot(u, wu_ref[...])
    part = _dot(g.astype(BF16), wd_ref[...])

    @pl.when(j == 0)
    def _():
        o_ref[...] = part

    @pl.when(j > 0)
    def _():
        o_ref[...] += part

    @pl.when(j == pl.num_programs(1) - 1)
    def _():
        fw = fw_ref[...]
        for r0 in range(0, tm, ROW_CHUNK):
            rows = slice(r0, r0 + ROW_CHUNK)
            is_ctx = _is_ctx_rows(i, tpb, tm, s_lat, r0, ROW_CHUNK)
            gate = jnp.where(is_ctx, mc_ref[0, 5:6], ml_ref[0, 5:6])
            out = h_ref[rows, :] + gate * o_ref[rows, :]
            if final_norm:
                out = _rms(out, fw)
            o_ref[rows, :] = out


def _ffn(h, nw, mods, wg, wu, wd, fw, *, s_lat, s_tot, final_norm):
    r, d = h.shape
    hid = wg.shape[1]
    tm, th = ROW_TILE, 512
    tpb = s_tot // tm
    kern = functools.partial(_ffn_kernel, tm=tm, tpb=tpb, s_lat=s_lat, final_norm=final_norm)
    return pl.pallas_call(
        kern,
        grid=(r // tm, hid // th),
        in_specs=[
            pl.BlockSpec((tm, d), lambda i, j: (i, 0)),
            pl.BlockSpec((1, d), lambda i, j: (0, 0)),
            pl.BlockSpec((1, 6, d), lambda i, j: (i // tpb, 0, 0)),
            pl.BlockSpec((1, 6, d), lambda i, j: (2, 0, 0)),
            pl.BlockSpec((d, th), lambda i, j: (0, j)),
            pl.BlockSpec((d, th), lambda i, j: (0, j)),
            pl.BlockSpec((th, d), lambda i, j: (j, 0)),
            pl.BlockSpec((1, d), lambda i, j: (0, 0)),
        ],
        out_specs=pl.BlockSpec((tm, d), lambda i, j: (i, 0)),
        out_shape=jax.ShapeDtypeStruct((r, d), F32),
        scratch_shapes=[pltpu.VMEM((tm, d), BF16)],
        compiler_params=_params(("arbitrary", "arbitrary")),
        name="ffn",
    )(h, nw, mods, mods, wg, wu, wd, fw)


def _rope_tables(s_lat, n_ctx, dim):
    quarter = dim // 4
    inv = ROPE_THETA ** (-jnp.arange(quarter, dtype=F32) / quarter)
    t = jnp.arange(s_lat)
    ang_r = (t // GRID_W).astype(F32)[:, None] * inv
    ang_c = (t % GRID_W).astype(F32)[:, None] * inv
    cos = jnp.concatenate([jnp.cos(ang_r)] * 2 + [jnp.cos(ang_c)] * 2, axis=1)
    sin = jnp.concatenate([-jnp.sin(ang_r), jnp.sin(ang_r), -jnp.sin(ang_c), jnp.sin(ang_c)], axis=1)
    cos = jnp.concatenate([cos, jnp.ones((n_ctx, dim), F32)], axis=0)
    sin = jnp.concatenate([sin, jnp.zeros((n_ctx, dim), F32)], axis=0)
    padw = LANE - dim
    if padw:
        cos = jnp.pad(cos, ((0, 0), (0, padw)))
        sin = jnp.pad(sin, ((0, 0), (0, padw)))
    return cos, sin


def _pad_w_in(w):
    d = w.shape[0]
    z = lambda n: jnp.zeros((d, n), w.dtype)
    ckv_kr = w[:, 0:576]
    gk_gv = w[:, 576:1600]
    rx = w[:, 1600:3648]
    cq = w[:, 3648:4160]
    gq = w[:, 4160:6208]
    ry = w[:, 6208:8256]
    gate = w[:, 8256:14400]
    return jnp.concatenate([ckv_kr, z(64), gk_gv, z(384), rx, gq, ry, gate, cq], axis=1).astype(BF16)


def _pad_q_up(w):
    rank = w.shape[0]
    w = w.reshape(rank, MLA_HEADS, MLA_NOPE + MLA_ROPE)
    w = jnp.pad(w, ((0, 0), (0, 0), (0, MLA_QK_PAD - MLA_NOPE - MLA_ROPE)))
    return w.reshape(rank, MLA_HEADS * MLA_QK_PAD).astype(BF16)


def kernel(x, c, ctx, c_ctx, mod_w, mod_b, norm_mix_w, norm_ffn_w, w_in, mla_q_norm_w, mla_q_up,
           mla_kv_norm_w, mla_kv_up, mla_out, gqa_q_norm_w, gqa_k_norm_w, gqa_out, rg_conv_w, rg_conv_b,
           rg_a_w, rg_a_b, rg_x_w, rg_x_b, rg_lambda, rg_out, merge_out, ffn_w_gate, ffn_w_up, ffn_w_down,
           final_norm_w):
    batch, s_lat, d = x.shape
    n_ctx = ctx.shape[1]
    depth = mod_w.shape[0]
    s_tot = s_lat + n_ctx
    assert batch == 2 and d == D_MODEL and n_ctx == Q_TILE
    assert s_tot % ROW_TILE == 0 and s_lat % KV_CHUNK == 0 and s_lat % SCAN_TILE == 0 and s_lat % GRID_W == 0

    h = jnp.concatenate([x, ctx], axis=1).reshape(batch * s_tot, d)
    cs = jnp.zeros((8, d), F32).at[0:batch].set(c).at[batch].set(c_ctx)
    mods_all = _mods(cs, mod_w, mod_b).reshape(depth, 8, 6, d)
    tabs = _rope_tables(s_lat, n_ctx, MLA_ROPE) + _rope_tables(s_lat, n_ctx, GQA_HD)
    row2 = lambda v: v.reshape(1, -1)
    dims = dict(s_lat=s_lat, s_tot=s_tot)

    for l in range(depth):
        last = l == depth - 1
        mods = mods_all[l]
        lp = {
            "mla_kv_norm_w": row2(mla_kv_norm_w[l]), "mla_q_norm_w": row2(mla_q_norm_w[l]),
            "mla_kv_up": mla_kv_up[l].astype(BF16), "mla_q_up": _pad_q_up(mla_q_up[l]),
            "gqa_k_norm_w": row2(gqa_k_norm_w[l]), "gqa_q_norm_w": row2(gqa_q_norm_w[l]),
            "rg_conv_w": rg_conv_w[l], "rg_conv_b": row2(rg_conv_b[l]),
            "rg_a_w": rg_a_w[l].astype(BF16), "rg_a_b": rg_a_b[l],
            "rg_x_w": rg_x_w[l].astype(BF16), "rg_x_b": rg_x_b[l], "rg_lambda": rg_lambda[l],
            "mla_out": mla_out[l].astype(BF16), "gqa_out": gqa_out[l].astype(BF16),
            "rg_out": rg_out[l].astype(BF16),
        }
        z = _inproj(h, row2(norm_mix_w[l]), mods, _pad_w_in(w_in[l]), **dims)
        kmla, vtm, qtm, kg, vtg, qtg = _prep(z, lp, tabs, s_tot=s_tot)
        o_mla = _attention(qtm, kmla, vtm, heads=1, batch=batch, name="mla_attention", **dims)
        o_gqa = _attention(qtg, kg, vtg, heads=GQA_GROUP, batch=batch, name="gqa_attention", **dims)
        rg = _rglru(z, lp, batch=batch, **dims)
        y = _merge(o_mla, o_gqa, rg, z, lp)
        h = _merge_out(y, merge_out[l].astype(BF16), h, mods, **dims)
        h = _ffn(h, row2(norm_ffn_w[l]), mods, ffn_w_gate[l].astype(BF16), ffn_w_up[l].astype(BF16),
                 ffn_w_down[l].astype(BF16), row2(final_norm_w), final_norm=last, **dims)

    return h.reshape(batch, s_tot, d)[:, :s_lat]
```

```python
import functools
import math

import jax
import jax.numpy as jnp
from jax import lax
from jax.experimental import pallas as pl
from jax.experimental.pallas import tpu as pltpu

F32 = jnp.float32
BF16 = jnp.bfloat16

D_MODEL = 2048
GRID_W = 64
ROPE_THETA = 10000.0
NORM_EPS = 1e-6

MLA_HEADS = 16
MLA_RANK = 512
MLA_NOPE = 128
MLA_ROPE = 64
MLA_V = 128
MLA_QK_PAD = 256

GQA_HEADS = 16
GQA_KV = 4
GQA_GROUP = GQA_HEADS // GQA_KV
GQA_HD = 128

RNN_BLOCKS = 8
RNN_BW = D_MODEL // RNN_BLOCKS
CONV_W = 4
LRU_C = 8.0
FFN_HIDDEN = 5632

C_CKV = 0
C_KR = 512
C_GK = 640
C_GV = 1152
C_RX = 2048
C_GQ = 4096
C_RY = 6144
C_GATE = 8192
C_CQ = 14336
Z_COLS = 14848

LANE = 128
VMEM_CAP = 64 * 1024 * 1024
VMEM_LIMIT = 56 * 1024 * 1024

ROW_TILE = 768
ROW_CHUNK = 128
PREP_TILE = 384
Q_STRIP = 256
GQA_Q_TILE = 512
MLA_Q_TILE = 2048
KV_CHUNK = 768
VT_ROWS = 144
SCAN_TILE = 256
LOG2E = 1.4426950408889634


def _params(sem):
    return pltpu.CompilerParams(dimension_semantics=sem, vmem_limit_bytes=VMEM_LIMIT)


def _dot(a, b):
    return jnp.dot(a, b, preferred_element_type=F32)


def _rms(x, w):
    return x * lax.rsqrt(jnp.mean(x * x, axis=-1, keepdims=True) + NORM_EPS) * w


def _is_ctx_rows(tile_idx, tiles_per_batch, tm, s_lat, r0=0, n=None):
    n = tm if n is None else n
    r = (tile_idx % tiles_per_batch) * tm + r0 + lax.broadcasted_iota(jnp.int32, (n, 1), 0)
    return r >= s_lat


def _mod_norm_to(u_scr, h_ref, nw_ref, ml_ref, mc_ref, tile_idx, tpb, tm, s_lat, k_shift, k_scale):
    nw = nw_ref[...]
    for r0 in range(0, tm, ROW_CHUNK):
        is_ctx = _is_ctx_rows(tile_idx, tpb, tm, s_lat, r0, ROW_CHUNK)
        y = _rms(h_ref[r0:r0 + ROW_CHUNK, :], nw)
        shift = jnp.where(is_ctx, mc_ref[0, k_shift:k_shift + 1], ml_ref[0, k_shift:k_shift + 1])
        scale = jnp.where(is_ctx, mc_ref[0, k_scale:k_scale + 1], ml_ref[0, k_scale:k_scale + 1])
        u_scr[r0:r0 + ROW_CHUNK, :] = (y * (1.0 + scale) + shift).astype(BF16)


def _rope(x, cos, sin_signed, half):
    n = x.shape[-1]
    lane = lax.broadcasted_iota(jnp.int32, x.shape, 1)
    up = pltpu.roll(x, n - half, axis=1)
    dn = pltpu.roll(x, half, axis=1)
    partner = jnp.where((lane % (2 * half)) < half, up, dn)
    return x * cos + partner * sin_signed


def _mods_kernel(c_ref, w_ref, b_ref, o_ref):
    x = c_ref[...]
    x = x * jax.nn.sigmoid(x)
    w = w_ref[0]
    xh = x.astype(BF16)
    xl = (x - xh.astype(F32)).astype(BF16)
    wh = w.astype(BF16)
    wl = (w - wh.astype(F32)).astype(BF16)
    o_ref[0] = _dot(xh, wh) + _dot(xl, wh) + _dot(xh, wl) + b_ref[0]


def _mods(cs, mod_w, mod_b):
    depth, d, n = mod_w.shape
    tn = 512
    return pl.pallas_call(
        _mods_kernel,
        grid=(depth, n // tn),
        in_specs=[
            pl.BlockSpec((8, d), lambda l, j: (0, 0)),
            pl.BlockSpec((1, d, tn), lambda l, j: (l, 0, j)),
            pl.BlockSpec((1, 1, tn), lambda l, j: (l, 0, j)),
        ],
        out_specs=pl.BlockSpec((1, 8, tn), lambda l, j: (l, 0, j)),
        out_shape=jax.ShapeDtypeStruct((depth, 8, n), F32),
        compiler_params=_params(("arbitrary", "arbitrary")),
        name="adaln_mods",
    )(cs, mod_w, mod_b.reshape(depth, 1, n))


def _inproj_kernel(h_ref, nw_ref, ml_ref, mc_ref, w_ref, z_ref, u_scr, *, tm, tpb, s_lat):
    i = pl.program_id(0)

    @pl.when(pl.program_id(1) == 0)
    def _():
        _mod_norm_to(u_scr, h_ref, nw_ref, ml_ref, mc_ref, i, tpb, tm, s_lat, 0, 1)

    z_ref[...] = _dot(u_scr[...], w_ref[...]).astype(z_ref.dtype)


def _inproj(h, nw, mods, w_pad, *, s_lat, s_tot):
    r, d = h.shape
    tm, tn = ROW_TILE, 512
    tpb = s_tot // tm
    return pl.pallas_call(
        functools.partial(_inproj_kernel, tm=tm, tpb=tpb, s_lat=s_lat),
        grid=(r // tm, Z_COLS // tn),
        in_specs=[
            pl.BlockSpec((tm, d), lambda i, j: (i, 0)),
            pl.BlockSpec((1, d), lambda i, j: (0, 0)),
            pl.BlockSpec((1, 6, d), lambda i, j: (i // tpb, 0, 0)),
            pl.BlockSpec((1, 6, d), lambda i, j: (2, 0, 0)),
            pl.BlockSpec((d, tn), lambda i, j: (0, j)),
        ],
        out_specs=pl.BlockSpec((tm, tn), lambda i, j: (i, j)),
        out_shape=jax.ShapeDtypeStruct((r, Z_COLS), BF16),
        scratch_shapes=[pltpu.VMEM((tm, d), BF16)],
        compiler_params=_params(("arbitrary", "arbitrary")),
        name="in_proj",
    )(h, nw, mods, mods, w_pad)


def _prep_kernel(za_ref, cq_ref, gq_ref, kvw_ref, qw_ref, kvup_ref, qup_ref, gkw_ref, gqw_ref,
                 cm_ref, sm_ref, cg_ref, sg_ref,
                 kmla_ref, vtm_ref, qtm_ref, kg_ref, vtg_ref, qtg_ref):
    cos_m, sin_m = cm_ref[...], sm_ref[...]
    cos_g, sin_g = cg_ref[...], sg_ref[...]
    mla_scale = (MLA_NOPE + MLA_ROPE) ** -0.5 * LOG2E
    gqa_scale = GQA_HD ** -0.5 * LOG2E

    ckv = _rms(za_ref[:, C_CKV:C_CKV + MLA_RANK].astype(F32), kvw_ref[...]).astype(BF16)
    kr = _rope(za_ref[:, C_KR:C_KR + LANE].astype(F32), cos_m, sin_m, MLA_ROPE // 4).astype(BF16)
    ones = jnp.ones((VT_ROWS - GQA_HD, za_ref.shape[0]), BF16)
    for h in range(MLA_HEADS):
        kv = _dot(ckv, kvup_ref[:, h * 256:(h + 1) * 256])
        kmla_ref[0, h, :, 0:MLA_NOPE] = kv[:, 0:MLA_NOPE].astype(BF16)
        kmla_ref[0, h, :, MLA_NOPE:MLA_QK_PAD] = kr
        vtm_ref[0, h, 0:MLA_V, :] = kv[:, MLA_NOPE:].T.astype(BF16)
        vtm_ref[0, h, MLA_V:VT_ROWS, :] = ones

    cq = _rms(cq_ref[...].astype(F32), qw_ref[...]).astype(BF16)
    for h in range(MLA_HEADS):
        q = _dot(cq, qup_ref[:, h * MLA_QK_PAD:(h + 1) * MLA_QK_PAD])
        q_rope = _rope(q[:, MLA_NOPE:], cos_m, sin_m, MLA_ROPE // 4)
        qtm_ref[0, h, 0:MLA_NOPE, :] = (q[:, 0:MLA_NOPE] * mla_scale).T.astype(BF16)
        qtm_ref[0, h, MLA_NOPE:MLA_QK_PAD, :] = (q_rope * mla_scale).T.astype(BF16)

    for n in range(GQA_KV):
        k = _rms(za_ref[:, C_GK + n * GQA_HD:C_GK + (n + 1) * GQA_HD].astype(F32), gkw_ref[...])
        kg_ref[0, n] = _rope(k, cos_g, sin_g, GQA_HD // 4).astype(BF16)
        v = za_ref[:, C_GV + n * GQA_HD:C_GV + (n + 1) * GQA_HD]
        vtg_ref[0, n, 0:GQA_HD, :] = v.astype(F32).T.astype(BF16)
        vtg_ref[0, n, GQA_HD:VT_ROWS, :] = ones
    for h in range(GQA_HEADS):
        q = _rms(gq_ref[:, h * GQA_HD:(h + 1) * GQA_HD].astype(F32), gqw_ref[...])
        q = _rope(q, cos_g, sin_g, GQA_HD // 4) * gqa_scale
        qtg_ref[0, h] = q.T.astype(BF16)


def _prep(z, lp, tabs, *, s_tot, batch):
    tm = PREP_TILE
    tpb = s_tot // tm
    cos_m, sin_m, cos_g, sin_g = tabs
    tab = lambda b, i: (i, 0)
    const = lambda b, i: (0, 0)
    rows_out = lambda b, i: (b, 0, i, 0)
    cols_out = lambda b, i: (b, 0, 0, i)
    outs = [
        jax.ShapeDtypeStruct((batch, MLA_HEADS, s_tot, MLA_QK_PAD), BF16),
        jax.ShapeDtypeStruct((batch, MLA_HEADS, VT_ROWS, s_tot), BF16),
        jax.ShapeDtypeStruct((batch, MLA_HEADS, MLA_QK_PAD, s_tot), BF16),
        jax.ShapeDtypeStruct((batch, GQA_KV, s_tot, GQA_HD), BF16),
        jax.ShapeDtypeStruct((batch, GQA_KV, VT_ROWS, s_tot), BF16),
        jax.ShapeDtypeStruct((batch, GQA_HEADS, GQA_HD, s_tot), BF16),
    ]
    return pl.pallas_call(
        _prep_kernel,
        grid=(batch, tpb),
        in_specs=[
            pl.BlockSpec((tm, 2048), lambda b, i: (b * tpb + i, 0)),
            pl.BlockSpec((tm, MLA_RANK), lambda b, i: (b * tpb + i, C_CQ // MLA_RANK)),
            pl.BlockSpec((tm, 2048), lambda b, i: (b * tpb + i, C_GQ // 2048)),
            pl.BlockSpec((1, MLA_RANK), const),
            pl.BlockSpec((1, MLA_RANK), const),
            pl.BlockSpec((MLA_RANK, MLA_HEADS * 256), const),
            pl.BlockSpec((MLA_RANK, MLA_HEADS * MLA_QK_PAD), const),
            pl.BlockSpec((1, GQA_HD), const),
            pl.BlockSpec((1, GQA_HD), const),
            pl.BlockSpec((tm, LANE), tab),
            pl.BlockSpec((tm, LANE), tab),
            pl.BlockSpec((tm, LANE), tab),
            pl.BlockSpec((tm, LANE), tab),
        ],
        out_specs=[
            pl.BlockSpec((1, MLA_HEADS, tm, MLA_QK_PAD), rows_out),
            pl.BlockSpec((1, MLA_HEADS, VT_ROWS, tm), cols_out),
            pl.BlockSpec((1, MLA_HEADS, MLA_QK_PAD, tm), cols_out),
            pl.BlockSpec((1, GQA_KV, tm, GQA_HD), rows_out),
            pl.BlockSpec((1, GQA_KV, VT_ROWS, tm), cols_out),
            pl.BlockSpec((1, GQA_HEADS, GQA_HD, tm), cols_out),
        ],
        out_shape=outs,
        compiler_params=_params(("arbitrary", "arbitrary")),
        name="attn_prep",
    )(z, z, z, lp["mla_kv_norm_w"], lp["mla_q_norm_w"], lp["mla_kv_up"], lp["mla_q_up"],
      lp["gqa_k_norm_w"], lp["gqa_q_norm_w"], cos_m, sin_m, cos_g, sin_g)


def _softmax_strip(s, m_old):
    m_new = jnp.maximum(m_old, jnp.max(s, axis=0, keepdims=True))
    alpha = jnp.exp2(m_old - m_new)
    p = jnp.exp2((s - m_new).astype(BF16))
    return m_new, alpha, p


def _attn_kernel(qt_ref, k_ref, vt_ref, o_ref, acc_scr, m_scr, s_scr, *, heads, tq, n_chunks):
    ch = KV_CHUNK
    n_strips = heads * tq // Q_STRIP

    def q_strip(g):
        h, off = divmod(g * Q_STRIP, tq)
        return qt_ref[0, h, :, off:off + Q_STRIP]

    def keys(c):
        return k_ref[0, 0, pl.ds(pl.multiple_of(c * ch, ch), ch), :]

    m_scr[...] = jnp.full(m_scr.shape, -1e30, F32)
    acc_scr[...] = jnp.zeros(acc_scr.shape, F32)

    def chunk(c, issue_next):
        k = keys(c)
        vt = vt_ref[0, 0, :, pl.ds(pl.multiple_of(c * ch, ch), ch)]
        m_all = m_scr[...]
        scores = [s_scr[...]] + [None] * (n_strips - 1)
        probs = [None] * n_strips
        for t in range(n_strips + 1):
            if t + 1 < n_strips:
                scores[t + 1] = _dot(k, q_strip(t + 1))
            elif t + 1 == n_strips and issue_next:
                s_scr[...] = _dot(keys(c + 1), q_strip(0))
            if t < n_strips:
                cols = slice(t * Q_STRIP, (t + 1) * Q_STRIP)
                m_new, alpha, p = _softmax_strip(scores[t], m_all[:, cols])
                probs[t] = (m_new, alpha, p)
            if t >= 1:
                cols = slice((t - 1) * Q_STRIP, t * Q_STRIP)
                m_new, alpha, p = probs[t - 1]
                m_scr[:, cols] = m_new
                acc_scr[:, cols] = alpha * acc_scr[:, cols] + _dot(vt, p)

    s_scr[...] = _dot(keys(0), q_strip(0))

    def body(c, carry):
        chunk(c, True)
        return carry

    lax.fori_loop(0, n_chunks - 1, body, 0)
    chunk(n_chunks - 1, False)

    inv_l = 1.0 / acc_scr[GQA_HD:GQA_HD + 1, :]
    for g in range(heads):
        for j in range(tq // Q_STRIP):
            cols = slice(g * tq + j * Q_STRIP, g * tq + (j + 1) * Q_STRIP)
            o = acc_scr[0:GQA_HD, cols] * inv_l[:, cols]
            o_ref[0, j * Q_STRIP:(j + 1) * Q_STRIP, g * GQA_HD:(g + 1) * GQA_HD] = o.T.astype(o_ref.dtype)


def _ctx_attn_kernel(qt_ref, k_ref, vt_ref, o_in_ref, o_ref, *, heads):
    del o_in_ref
    q = jnp.concatenate([qt_ref[0, g] for g in range(heads)], axis=1) if heads > 1 else qt_ref[0, 0]
    s = _dot(k_ref[0, 0], q)
    _, _, p = _softmax_strip(s, jnp.full((1, s.shape[1]), -1e30, F32))
    acc = _dot(vt_ref[0, 0], p)
    o = acc[0:GQA_HD] / acc[GQA_HD:GQA_HD + 1]
    n = o.shape[1] // heads
    for g in range(heads):
        o_ref[0, :, g * GQA_HD:(g + 1) * GQA_HD] = o[:, g * n:(g + 1) * n].T.astype(o_ref.dtype)


def _attention(qt, k, vt, *, heads, tq, s_lat, s_tot, name):
    batch, n_heads, dk, _ = qt.shape
    n_kv = k.shape[1]
    n_ctx = s_tot - s_lat
    tq = min(tq, s_lat)
    assert s_tot % KV_CHUNK == 0 and s_lat % tq == 0 and tq % Q_STRIP == 0 and s_lat % n_ctx == 0
    kern = functools.partial(_attn_kernel, heads=heads, tq=tq, n_chunks=s_tot // KV_CHUNK)
    out_shape = jax.ShapeDtypeStruct((batch, s_tot, n_heads * GQA_HD), BF16)
    o = pl.pallas_call(
        kern,
        grid=(batch, n_kv, s_lat // tq),
        in_specs=[
            pl.BlockSpec((1, heads, dk, tq), lambda b, n, q: (b, n, 0, q)),
            pl.BlockSpec((1, 1, s_tot, dk), lambda b, n, q: (b, n, 0, 0)),
            pl.BlockSpec((1, 1, VT_ROWS, s_tot), lambda b, n, q: (b, n, 0, 0)),
        ],
        out_specs=pl.BlockSpec((1, tq, heads * GQA_HD), lambda b, n, q: (b, q, n)),
        out_shape=out_shape,
        scratch_shapes=[
            pltpu.VMEM((VT_ROWS, heads * tq), F32),
            pltpu.VMEM((1, heads * tq), F32),
            pltpu.VMEM((KV_CHUNK, Q_STRIP), F32),
        ],
        compiler_params=_params(("arbitrary", "arbitrary", "arbitrary")),
        name=name,
    )(qt, k, vt)
    ctx_blk = s_lat // n_ctx
    return pl.pallas_call(
        functools.partial(_ctx_attn_kernel, heads=heads),
        grid=(batch, n_kv),
        in_specs=[
            pl.BlockSpec((1, heads, dk, n_ctx), lambda b, n: (b, n, 0, ctx_blk)),
            pl.BlockSpec((1, 1, n_ctx, dk), lambda b, n: (b, n, ctx_blk, 0)),
            pl.BlockSpec((1, 1, VT_ROWS, n_ctx), lambda b, n: (b, n, 0, ctx_blk)),
            pl.BlockSpec(memory_space=pl.ANY),
        ],
        out_specs=pl.BlockSpec((1, n_ctx, heads * GQA_HD), lambda b, n: (b, ctx_blk, n)),
        out_shape=out_shape,
        input_output_aliases={3: 0},
        compiler_params=_params(("arbitrary", "arbitrary")),
        name=name + "_ctx",
    )(qt, k, vt, o)


def _shift_rows(x, s, fill, reverse):
    n = x.shape[0]
    t = lax.broadcasted_iota(jnp.int32, x.shape, 0)
    if reverse:
        return jnp.where(t < n - s, pltpu.roll(x, n - s, axis=0), fill)
    return jnp.where(t >= s, pltpu.roll(x, s, axis=0), fill)


def _tile_scan(a, b, h0, reverse):
    s = 1
    while s < a.shape[0]:
        b = a * _shift_rows(b, s, 0.0, reverse) + b
        a = a * _shift_rows(a, s, 1.0, reverse)
        s *= 2
    return a * h0 + b


def _gelu_tanh(x):
    return 0.5 * x * (1.0 + jnp.tanh(math.sqrt(2.0 / math.pi) * (x + 0.044715 * x * x * x)))


def _rglru_kernel(rx_ref, ry_ref, cw_ref, cb_ref, wa_ref, ba_ref, wx_ref, bx_ref, lam_ref,
                  o_ref, xp_scr, hf_scr, *, s_lat, n_ctx):
    tt = SCAN_TILE
    pad = 8
    lat0 = pad
    ctx0 = pad + s_lat + pad
    zeros = jnp.zeros((pad, RNN_BW), F32)
    xp_scr[0:pad, :] = zeros
    xp_scr[lat0 + s_lat:ctx0, :] = zeros
    xp_scr[ctx0 + n_ctx:ctx0 + n_ctx + pad, :] = zeros
    xp_scr[lat0:lat0 + s_lat, :] = rx_ref[0:s_lat, :].astype(F32)
    xp_scr[ctx0:ctx0 + n_ctx, :] = rx_ref[s_lat:s_lat + n_ctx, :].astype(F32)

    cw = cw_ref[...]
    cb = cb_ref[...]
    lam = -lam_ref[...]
    softplus = jnp.log1p(jnp.exp(-jnp.abs(lam))) + jnp.maximum(lam, 0.0)

    def coeffs(xp0, n, d):
        ext = xp_scr[pl.ds(xp0 - pad, n + 2 * pad), :]
        xc = cb
        for j in range(CONV_W):
            o = pad - CONV_W // 2 + j
            xc = xc + ext[o:o + n, :] * cw[j:j + 1, :]
        xb = xc.astype(BF16)
        r = jax.nn.sigmoid(_dot(xb, wa_ref[d, 0]) + ba_ref[d:d + 1, :])
        i = jax.nn.sigmoid(_dot(xb, wx_ref[d, 0]) + bx_ref[d:d + 1, :])
        log_a = -LRU_C * r * softplus[d:d + 1, :]
        a = jnp.exp(log_a)
        mult = jnp.sqrt(jnp.maximum(jnp.tanh(-log_a) * (1.0 + a * a), 0.0))
        return a, mult * i * xc

    def scan_tile(row0, xp0, n, d, h0):
        a, b = coeffs(xp0, n, d)
        h = _tile_scan(a, b, h0, reverse=(d == 1))
        if d == 0:
            hf_scr[pl.ds(row0, n), :] = h
            return h[n - 1:n, :]
        y = ry_ref[pl.ds(row0, n), :].astype(F32)
        o_ref[pl.ds(row0, n), :] = ((hf_scr[pl.ds(row0, n), :] + h) * _gelu_tanh(y)).astype(o_ref.dtype)
        return h[0:1, :]

    n_lat_tiles = s_lat // tt
    h0 = jnp.zeros((1, RNN_BW), F32)

    hc = scan_tile(s_lat, ctx0, n_ctx, 0, h0)

    def fwd(t, h):
        r0 = pl.multiple_of(t * tt, tt)
        return scan_tile(r0, r0 + lat0, tt, 0, h)

    lax.fori_loop(0, n_lat_tiles, fwd, hc)

    hc = scan_tile(s_lat, ctx0, n_ctx, 1, h0)

    def bwd(t, h):
        r0 = pl.multiple_of((n_lat_tiles - 1 - t) * tt, tt)
        return scan_tile(r0, r0 + lat0, tt, 1, h)

    lax.fori_loop(0, n_lat_tiles, bwd, hc)


def _rglru(z, lp, *, s_lat, s_tot, batch):
    r = z.shape[0]
    n_ctx = s_tot - s_lat
    bw = RNN_BW
    kern = functools.partial(_rglru_kernel, s_lat=s_lat, n_ctx=n_ctx)
    vec = lambda b, k: (0, k)
    return pl.pallas_call(
        kern,
        grid=(batch, RNN_BLOCKS),
        in_specs=[
            pl.BlockSpec((s_tot, bw), lambda b, k: (b, C_RX // bw + k)),
            pl.BlockSpec((s_tot, bw), lambda b, k: (b, C_RY // bw + k)),
            pl.BlockSpec((CONV_W, bw), vec),
            pl.BlockSpec((1, bw), vec),
            pl.BlockSpec((2, 1, bw, bw), lambda b, k: (0, k, 0, 0)),
            pl.BlockSpec((2, bw), vec),
            pl.BlockSpec((2, 1, bw, bw), lambda b, k: (0, k, 0, 0)),
            pl.BlockSpec((2, bw), vec),
            pl.BlockSpec((2, bw), vec),
        ],
        out_specs=pl.BlockSpec((s_tot, bw), lambda b, k: (b, k)),
        out_shape=jax.ShapeDtypeStruct((r, D_MODEL), BF16),
        scratch_shapes=[
            pltpu.VMEM((s_tot + 24, bw), F32),
            pltpu.VMEM((s_tot, bw), F32),
        ],
        compiler_params=_params(("arbitrary", "arbitrary")),
        name="rglru",
    )(z, z, lp["rg_conv_w"], lp["rg_conv_b"], lp["rg_a_w"], lp["rg_a_b"], lp["rg_x_w"], lp["rg_x_b"],
      lp["rg_lambda"])


def _merge_kernel(om_ref, og_ref, rg_ref, gm_ref, gg_ref, gr_ref, wm_ref, wg_ref, wr_ref, y_ref):
    y = (jax.nn.sigmoid(gm_ref[...].astype(F32)) * _dot(om_ref[...], wm_ref[...])
         + jax.nn.sigmoid(gg_ref[...].astype(F32)) * _dot(og_ref[...], wg_ref[...])
         + jax.nn.sigmoid(gr_ref[...].astype(F32)) * _dot(rg_ref[...], wr_ref[...]))
    y_ref[...] = y.astype(y_ref.dtype)


def _merge(o_mla, o_gqa, rg, z, lp):
    r, d = o_mla.shape
    tm, tn = ROW_TILE, 512
    gate0 = C_GATE // tn
    per = d // tn
    act = pl.BlockSpec((tm, d), lambda i, j: (i, 0))
    wgt = pl.BlockSpec((d, tn), lambda i, j: (0, j))
    return pl.pallas_call(
        _merge_kernel,
        grid=(r // tm, d // tn),
        in_specs=[
            act, act, act,
            pl.BlockSpec((tm, tn), lambda i, j: (i, gate0 + j)),
            pl.BlockSpec((tm, tn), lambda i, j: (i, gate0 + per + j)),
            pl.BlockSpec((tm, tn), lambda i, j: (i, gate0 + 2 * per + j)),
            wgt, wgt, wgt,
        ],
        out_specs=pl.BlockSpec((tm, tn), lambda i, j: (i, j)),
        out_shape=jax.ShapeDtypeStruct((r, d), BF16),
        compiler_params=_params(("arbitrary", "arbitrary")),
        name="merge_branches",
    )(o_mla, o_gqa, rg, z, z, z, lp["mla_out"], lp["gqa_out"], lp["rg_out"])


def _resid_kernel(y_ref, w_ref, h_ref, ml_ref, mc_ref, o_ref, *, tm, tpb, s_lat, tn):
    i, j = pl.program_id(0), pl.program_id(1)
    is_ctx = _is_ctx_rows(i, tpb, tm, s_lat)
    col = pl.ds(pl.multiple_of(j * tn, tn), tn)
    gate = jnp.where(is_ctx, mc_ref[0, 2:3, col], ml_ref[0, 2:3, col])
    o_ref[...] = h_ref[...] + gate * _dot(y_ref[...], w_ref[...])


def _merge_out(y, w, h, mods, *, s_lat, s_tot):
    r, d = h.shape
    tm, tn = ROW_TILE, 512
    tpb = s_tot // tm
    return pl.pallas_call(
        functools.partial(_resid_kernel, tm=tm, tpb=tpb, s_lat=s_lat, tn=tn),
        grid=(r // tm, d // tn),
        in_specs=[
            pl.BlockSpec((tm, d), lambda i, j: (i, 0)),
            pl.BlockSpec((d, tn), lambda i, j: (0, j)),
            pl.BlockSpec((tm, tn), lambda i, j: (i, j)),
            pl.BlockSpec((1, 6, d), lambda i, j: (i // tpb, 0, 0)),
            pl.BlockSpec((1, 6, d), lambda i, j: (2, 0, 0)),
        ],
        out_specs=pl.BlockSpec((tm, tn), lambda i, j: (i, j)),
        out_shape=jax.ShapeDtypeStruct((r, d), F32),
        compiler_params=_params(("arbitrary", "arbitrary")),
        name="merge_out_residual",
    )(y, w, h, mods, mods)


def _ffn_kernel(h_ref, nw_ref, ml_ref, mc_ref, wg_ref, wu_ref, wd_ref, fw_ref, o_ref, u_scr,
                *, tm, tpb, s_lat, final_norm):
    i, j = pl.program_id(0), pl.program_id(1)

    @pl.when(j == 0)
    def _():
        _mod_norm_to(u_scr, h_ref, nw_ref, ml_ref, mc_ref, i, tpb, tm, s_lat, 3, 4)

    u = u_scr[...]
    g = _dot(u, wg_ref[...])
    g = g * jax.nn.sigmoid(g) * _dot(u, wu_ref[...])
    part = _dot(g.astype(BF16), wd_ref[...])

    @pl.when(j == 0)
    def _():
        o_ref[...] = part

    @pl.when(j > 0)
    def _():
        o_ref[...] += part

    @pl.when(j == pl.num_programs(1) - 1)
    def _():
        fw = fw_ref[...]
        for r0 in range(0, tm, ROW_CHUNK):
            rows = slice(r0, r0 + ROW_CHUNK)
            is_ctx = _is_ctx_rows(i, tpb, tm, s_lat, r0, ROW_CHUNK)
            gate = jnp.where(is_ctx, mc_ref[0, 5:6], ml_ref[0, 5:6])
            out = h_ref[rows, :] + gate * o_ref[rows, :]
            if final_norm:
                out = _rms(out, fw)
            o_ref[rows, :] = out


def _ffn(h, nw, mods, wg, wu, wd, fw, *, s_lat, s_tot, final_norm):
    r, d = h.shape
    hid = wg.shape[1]
    tm, th = ROW_TILE, 512
    tpb = s_tot // tm
    kern = functools.partial(_ffn_kernel, tm=tm, tpb=tpb, s_lat=s_lat, final_norm=final_norm)
    return pl.pallas_call(
        kern,
        grid=(r // tm, hid // th),
        in_specs=[
            pl.BlockSpec((tm, d), lambda i, j: (i, 0)),
            pl.BlockSpec((1, d), lambda i, j: (0, 0)),
            pl.BlockSpec((1, 6, d), lambda i, j: (i // tpb, 0, 0)),
            pl.BlockSpec((1, 6, d), lambda i, j: (2, 0, 0)),
            pl.BlockSpec((d, th), lambda i, j: (0, j)),
            pl.BlockSpec((d, th), lambda i, j: (0, j)),
            pl.BlockSpec((th, d), lambda i, j: (j, 0)),
            pl.BlockSpec((1, d), lambda i, j: (0, 0)),
        ],
        out_specs=pl.BlockSpec((tm, d), lambda i, j: (i, 0)),
        out_shape=jax.ShapeDtypeStruct((r, d), F32),
        scratch_shapes=[pltpu.VMEM((tm, d), BF16)],
        compiler_params=_params(("arbitrary", "arbitrary")),
        name="ffn",
    )(h, nw, mods, mods, wg, wu, wd, fw)


def _rope_tables(s_lat, n_ctx, dim):
    quarter = dim // 4
    inv = ROPE_THETA ** (-jnp.arange(quarter, dtype=F32) / quarter)
    t = jnp.arange(s_lat)
    ang_r = (t // GRID_W).astype(F32)[:, None] * inv
    ang_c = (t % GRID_W).astype(F32)[:, None] * inv
    cos = jnp.concatenate([jnp.cos(ang_r)] * 2 + [jnp.cos(ang_c)] * 2, axis=1)
    sin = jnp.concatenate([-jnp.sin(ang_r), jnp.sin(ang_r), -jnp.sin(ang_c), jnp.sin(ang_c)], axis=1)
    cos = jnp.concatenate([cos, jnp.ones((n_ctx, dim), F32)], axis=0)
    sin = jnp.concatenate([sin, jnp.zeros((n_ctx, dim), F32)], axis=0)
    padw = LANE - dim
    if padw:
        cos = jnp.pad(cos, ((0, 0), (0, padw)))
        sin = jnp.pad(sin, ((0, 0), (0, padw)))
    return cos, sin


def _pad_w_in(w):
    d = w.shape[0]
    z = lambda n: jnp.zeros((d, n), w.dtype)
    ckv_kr = w[:, 0:576]
    gk_gv = w[:, 576:1600]
    rx = w[:, 1600:3648]
    cq = w[:, 3648:4160]
    gq = w[:, 4160:6208]
    ry = w[:, 6208:8256]
    gate = w[:, 8256:14400]
    return jnp.concatenate([ckv_kr, z(64), gk_gv, z(384), rx, gq, ry, gate, cq], axis=1).astype(BF16)


def _pad_q_up(w):
    rank = w.shape[0]
    w = w.reshape(rank, MLA_HEADS, MLA_NOPE + MLA_ROPE)
    w = jnp.pad(w, ((0, 0), (0, 0), (0, MLA_QK_PAD - MLA_NOPE - MLA_ROPE)))
    return w.reshape(rank, MLA_HEADS * MLA_QK_PAD).astype(BF16)


def kernel(x, c, ctx, c_ctx, mod_w, mod_b, norm_mix_w, norm_ffn_w, w_in, mla_q_norm_w, mla_q_up,
           mla_kv_norm_w, mla_kv_up, mla_out, gqa_q_norm_w, gqa_k_norm_w, gqa_out, rg_conv_w, rg_conv_b,
           rg_a_w, rg_a_b, rg_x_w, rg_x_b, rg_lambda, rg_out, merge_out, ffn_w_gate, ffn_w_up, ffn_w_down,
           final_norm_w):
    batch, s_lat, d = x.shape
    n_ctx = ctx.shape[1]
    depth = mod_w.shape[0]
    s_tot = s_lat + n_ctx
    assert batch == 2 and d == D_MODEL and n_ctx == SCAN_TILE
    assert s_tot % ROW_TILE == 0 and s_tot % PREP_TILE == 0 and s_lat % SCAN_TILE == 0 and s_lat % GRID_W == 0

    h = jnp.concatenate([x, ctx], axis=1).reshape(batch * s_tot, d)
    cs = jnp.zeros((8, d), F32).at[0:batch].set(c).at[batch].set(c_ctx)
    mods_all = _mods(cs, mod_w, mod_b).reshape(depth, 8, 6, d)
    tabs = _rope_tables(s_lat, n_ctx, MLA_ROPE) + _rope_tables(s_lat, n_ctx, GQA_HD)
    row2 = lambda v: v.reshape(1, -1)
    dims = dict(s_lat=s_lat, s_tot=s_tot)

    for l in range(depth):
        last = l == depth - 1
        mods = mods_all[l]
        lp = {
            "mla_kv_norm_w": row2(mla_kv_norm_w[l]), "mla_q_norm_w": row2(mla_q_norm_w[l]),
            "mla_kv_up": mla_kv_up[l].astype(BF16), "mla_q_up": _pad_q_up(mla_q_up[l]),
            "gqa_k_norm_w": row2(gqa_k_norm_w[l]), "gqa_q_norm_w": row2(gqa_q_norm_w[l]),
            "rg_conv_w": rg_conv_w[l], "rg_conv_b": row2(rg_conv_b[l]),
            "rg_a_w": rg_a_w[l].astype(BF16), "rg_a_b": rg_a_b[l],
            "rg_x_w": rg_x_w[l].astype(BF16), "rg_x_b": rg_x_b[l], "rg_lambda": rg_lambda[l],
            "mla_out": mla_out[l].astype(BF16), "gqa_out": gqa_out[l].astype(BF16),
            "rg_out": rg_out[l].astype(BF16),
        }
        z = _inproj(h, row2(norm_mix_w[l]), mods, _pad_w_in(w_in[l]), **dims)
        kmla, vtm, qtm, kg, vtg, qtg = _prep(z, lp, tabs, s_tot=s_tot, batch=batch)
        o_mla = _attention(qtm, kmla, vtm, heads=1, tq=MLA_Q_TILE, name="mla_attention", **dims)
        o_gqa = _attention(qtg, kg, vtg, heads=GQA_GROUP, tq=GQA_Q_TILE, name="gqa_attention", **dims)
        o_mla = o_mla.reshape(batch * s_tot, d)
        o_gqa = o_gqa.reshape(batch * s_tot, d)
        rg = _rglru(z, lp, batch=batch, **dims)
        y = _merge(o_mla, o_gqa, rg, z, lp)
        h = _merge_out(y, merge_out[l].astype(BF16), h, mods, **dims)
        h = _ffn(h, row2(norm_ffn_w[l]), mods, ffn_w_gate[l].astype(BF16), ffn_w_up[l].astype(BF16),
                 ffn_w_down[l].astype(BF16), row2(final_norm_w), final_norm=last, **dims)

    return h.reshape(batch, s_tot, d)[:, :s_lat]
```

```python
import functools
import math

import jax
import jax.numpy as jnp
from jax import lax
from jax.experimental import pallas as pl
from jax.experimental.pallas import tpu as pltpu

F32 = jnp.float32
BF16 = jnp.bfloat16

D_MODEL = 2048
GRID_W = 64
ROPE_THETA = 10000.0
NORM_EPS = 1e-6

MLA_HEADS = 16
MLA_RANK = 512
MLA_NOPE = 128
MLA_ROPE = 64
MLA_V = 128
MLA_QK_PAD = 256

GQA_HEADS = 16
GQA_KV = 4
GQA_GROUP = GQA_HEADS // GQA_KV
GQA_HD = 128

RNN_BLOCKS = 8
RNN_BW = D_MODEL // RNN_BLOCKS
CONV_W = 4
LRU_C = 8.0
FFN_HIDDEN = 5632

C_CKV = 0
C_KR = 512
C_GK = 640
C_GV = 1152
C_RX = 2048
C_GQ = 4096
C_RY = 6144
C_GATE = 8192
C_CQ = 14336
Z_COLS = 14848

LANE = 128
SUBLANES = 8
VMEM_CAP = 64 * 1024 * 1024
VMEM_LIMIT = 56 * 1024 * 1024

ROW_TILE = 768
ROW_CHUNK = 128
PREP_TILE = 384
Q_STRIP = 256
GQA_Q_TILE = 1024
MLA_Q_TILE = 4096
KV_CHUNK = 768
VT_ROWS = 144
SCAN_TILE = 256
LOG2E = 1.4426950408889634


def _params(sem):
    return pltpu.CompilerParams(dimension_semantics=sem, vmem_limit_bytes=VMEM_LIMIT)


def _dot(a, b):
    return jnp.dot(a, b, preferred_element_type=F32)


def _rms(x, w):
    return x * lax.rsqrt(jnp.mean(x * x, axis=-1, keepdims=True) + NORM_EPS) * w


def _is_ctx_rows(tile_idx, tiles_per_batch, tm, s_lat, r0=0, n=None):
    n = tm if n is None else n
    r = (tile_idx % tiles_per_batch) * tm + r0 + lax.broadcasted_iota(jnp.int32, (n, 1), 0)
    return r >= s_lat


def _mod_norm_to(u_scr, h_ref, nw_ref, ml_ref, mc_ref, tile_idx, tpb, tm, s_lat, k_shift, k_scale):
    nw = nw_ref[...]
    for r0 in range(0, tm, ROW_CHUNK):
        is_ctx = _is_ctx_rows(tile_idx, tpb, tm, s_lat, r0, ROW_CHUNK)
        y = _rms(h_ref[r0:r0 + ROW_CHUNK, :], nw)
        shift = jnp.where(is_ctx, mc_ref[0, k_shift:k_shift + 1], ml_ref[0, k_shift:k_shift + 1])
        scale = jnp.where(is_ctx, mc_ref[0, k_scale:k_scale + 1], ml_ref[0, k_scale:k_scale + 1])
        u_scr[r0:r0 + ROW_CHUNK, :] = (y * (1.0 + scale) + shift).astype(BF16)


def _rope(x, cos, sin_signed, half):
    n = x.shape[-1]
    lane = lax.broadcasted_iota(jnp.int32, x.shape, 1)
    up = pltpu.roll(x, n - half, axis=1)
    dn = pltpu.roll(x, half, axis=1)
    partner = jnp.where((lane % (2 * half)) < half, up, dn)
    return x * cos + partner * sin_signed


def _mods_kernel(c_ref, w_ref, b_ref, o_ref):
    x = c_ref[...]
    x = x * jax.nn.sigmoid(x)
    w = w_ref[0]
    xh = x.astype(BF16)
    xl = (x - xh.astype(F32)).astype(BF16)
    wh = w.astype(BF16)
    wl = (w - wh.astype(F32)).astype(BF16)
    o_ref[0] = _dot(xh, wh) + _dot(xl, wh) + _dot(xh, wl) + b_ref[0]


def _mods(cs, mod_w, mod_b):
    depth, d, n = mod_w.shape
    tn = 512
    return pl.pallas_call(
        _mods_kernel,
        grid=(depth, n // tn),
        in_specs=[
            pl.BlockSpec((8, d), lambda l, j: (0, 0)),
            pl.BlockSpec((1, d, tn), lambda l, j: (l, 0, j)),
            pl.BlockSpec((1, 1, tn), lambda l, j: (l, 0, j)),
        ],
        out_specs=pl.BlockSpec((1, 8, tn), lambda l, j: (l, 0, j)),
        out_shape=jax.ShapeDtypeStruct((depth, 8, n), F32),
        compiler_params=_params(("arbitrary", "arbitrary")),
        name="adaln_mods",
    )(cs, mod_w, mod_b.reshape(depth, 1, n))


def _inproj_kernel(h_ref, nw_ref, ml_ref, mc_ref, w_ref, z_ref, u_scr, *, tm, tpb, s_lat):
    i = pl.program_id(0)

    @pl.when(pl.program_id(1) == 0)
    def _():
        _mod_norm_to(u_scr, h_ref, nw_ref, ml_ref, mc_ref, i, tpb, tm, s_lat, 0, 1)

    z_ref[...] = _dot(u_scr[...], w_ref[...]).astype(z_ref.dtype)


def _inproj(h, nw, mods, w_pad, *, s_lat, s_tot):
    r, d = h.shape
    tm, tn = ROW_TILE, 512
    tpb = s_tot // tm
    return pl.pallas_call(
        functools.partial(_inproj_kernel, tm=tm, tpb=tpb, s_lat=s_lat),
        grid=(r // tm, Z_COLS // tn),
        in_specs=[
            pl.BlockSpec((tm, d), lambda i, j: (i, 0)),
            pl.BlockSpec((1, d), lambda i, j: (0, 0)),
            pl.BlockSpec((1, 6, d), lambda i, j: (i // tpb, 0, 0)),
            pl.BlockSpec((1, 6, d), lambda i, j: (2, 0, 0)),
            pl.BlockSpec((d, tn), lambda i, j: (0, j)),
        ],
        out_specs=pl.BlockSpec((tm, tn), lambda i, j: (i, j)),
        out_shape=jax.ShapeDtypeStruct((r, Z_COLS), BF16),
        scratch_shapes=[pltpu.VMEM((tm, d), BF16)],
        compiler_params=_params(("arbitrary", "arbitrary")),
        name="in_proj",
    )(h, nw, mods, mods, w_pad)


def _prep_kernel(za_ref, cq_ref, gq_ref, kvw_ref, qw_ref, kvup_ref, qup_ref, gkw_ref, gqw_ref,
                 cm_ref, sm_ref, cg_ref, sg_ref,
                 kmla_ref, vtm_ref, qtm_ref, kg_ref, vtg_ref, qtg_ref):
    cos_m, sin_m = cm_ref[...], sm_ref[...]
    cos_g, sin_g = cg_ref[...], sg_ref[...]
    mla_scale = (MLA_NOPE + MLA_ROPE) ** -0.5 * LOG2E
    gqa_scale = GQA_HD ** -0.5 * LOG2E

    ckv = _rms(za_ref[:, C_CKV:C_CKV + MLA_RANK].astype(F32), kvw_ref[...]).astype(BF16)
    kr = _rope(za_ref[:, C_KR:C_KR + LANE].astype(F32), cos_m, sin_m, MLA_ROPE // 4).astype(BF16)
    ones = jnp.ones((VT_ROWS - GQA_HD, za_ref.shape[0]), BF16)
    for h in range(MLA_HEADS):
        kv = _dot(ckv, kvup_ref[:, h * 256:(h + 1) * 256])
        kmla_ref[0, h, :, 0:MLA_NOPE] = kv[:, 0:MLA_NOPE].astype(BF16)
        kmla_ref[0, h, :, MLA_NOPE:MLA_QK_PAD] = kr
        vtm_ref[0, h, 0:MLA_V, :] = kv[:, MLA_NOPE:].T.astype(BF16)
        vtm_ref[0, h, MLA_V:VT_ROWS, :] = ones

    cq = _rms(cq_ref[...].astype(F32), qw_ref[...]).astype(BF16)
    for h in range(MLA_HEADS):
        q = _dot(cq, qup_ref[:, h * MLA_QK_PAD:(h + 1) * MLA_QK_PAD])
        q_rope = _rope(q[:, MLA_NOPE:], cos_m, sin_m, MLA_ROPE // 4)
        qtm_ref[0, h, 0:MLA_NOPE, :] = (q[:, 0:MLA_NOPE] * mla_scale).T.astype(BF16)
        qtm_ref[0, h, MLA_NOPE:MLA_QK_PAD, :] = (q_rope * mla_scale).T.astype(BF16)

    for n in range(GQA_KV):
        k = _rms(za_ref[:, C_GK + n * GQA_HD:C_GK + (n + 1) * GQA_HD].astype(F32), gkw_ref[...])
        kg_ref[0, n] = _rope(k, cos_g, sin_g, GQA_HD // 4).astype(BF16)
        v = za_ref[:, C_GV + n * GQA_HD:C_GV + (n + 1) * GQA_HD]
        vtg_ref[0, n, 0:GQA_HD, :] = v.astype(F32).T.astype(BF16)
        vtg_ref[0, n, GQA_HD:VT_ROWS, :] = ones
    for h in range(GQA_HEADS):
        q = _rms(gq_ref[:, h * GQA_HD:(h + 1) * GQA_HD].astype(F32), gqw_ref[...])
        q = _rope(q, cos_g, sin_g, GQA_HD // 4) * gqa_scale
        qtg_ref[0, h] = q.T.astype(BF16)


def _prep(z, lp, tabs, *, s_tot, batch):
    tm = PREP_TILE
    tpb = s_tot // tm
    cos_m, sin_m, cos_g, sin_g = tabs
    tab = lambda b, i: (i, 0)
    const = lambda b, i: (0, 0)
    rows_out = lambda b, i: (b, 0, i, 0)
    cols_out = lambda b, i: (b, 0, 0, i)
    outs = [
        jax.ShapeDtypeStruct((batch, MLA_HEADS, s_tot, MLA_QK_PAD), BF16),
        jax.ShapeDtypeStruct((batch, MLA_HEADS, VT_ROWS, s_tot), BF16),
        jax.ShapeDtypeStruct((batch, MLA_HEADS, MLA_QK_PAD, s_tot), BF16),
        jax.ShapeDtypeStruct((batch, GQA_KV, s_tot, GQA_HD), BF16),
        jax.ShapeDtypeStruct((batch, GQA_KV, VT_ROWS, s_tot), BF16),
        jax.ShapeDtypeStruct((batch, GQA_HEADS, GQA_HD, s_tot), BF16),
    ]
    return pl.pallas_call(
        _prep_kernel,
        grid=(batch, tpb),
        in_specs=[
            pl.BlockSpec((tm, 2048), lambda b, i: (b * tpb + i, 0)),
            pl.BlockSpec((tm, MLA_RANK), lambda b, i: (b * tpb + i, C_CQ // MLA_RANK)),
            pl.BlockSpec((tm, 2048), lambda b, i: (b * tpb + i, C_GQ // 2048)),
            pl.BlockSpec((1, MLA_RANK), const),
            pl.BlockSpec((1, MLA_RANK), const),
            pl.BlockSpec((MLA_RANK, MLA_HEADS * 256), const),
            pl.BlockSpec((MLA_RANK, MLA_HEADS * MLA_QK_PAD), const),
            pl.BlockSpec((1, GQA_HD), const),
            pl.BlockSpec((1, GQA_HD), const),
            pl.BlockSpec((tm, LANE), tab),
            pl.BlockSpec((tm, LANE), tab),
            pl.BlockSpec((tm, LANE), tab),
            pl.BlockSpec((tm, LANE), tab),
        ],
        out_specs=[
            pl.BlockSpec((1, MLA_HEADS, tm, MLA_QK_PAD), rows_out),
            pl.BlockSpec((1, MLA_HEADS, VT_ROWS, tm), cols_out),
            pl.BlockSpec((1, MLA_HEADS, MLA_QK_PAD, tm), cols_out),
            pl.BlockSpec((1, GQA_KV, tm, GQA_HD), rows_out),
            pl.BlockSpec((1, GQA_KV, VT_ROWS, tm), cols_out),
            pl.BlockSpec((1, GQA_HEADS, GQA_HD, tm), cols_out),
        ],
        out_shape=outs,
        compiler_params=_params(("arbitrary", "arbitrary")),
        name="attn_prep",
    )(z, z, z, lp["mla_kv_norm_w"], lp["mla_q_norm_w"], lp["mla_kv_up"], lp["mla_q_up"],
      lp["gqa_k_norm_w"], lp["gqa_q_norm_w"], cos_m, sin_m, cos_g, sin_g)


def _softmax_strip(s, m_old):
    m_new = jnp.maximum(m_old, jnp.max(s, axis=0, keepdims=True).astype(F32))
    alpha = jnp.exp2(m_old - m_new)
    p = jnp.exp2(s - m_new.astype(BF16))
    return m_new, alpha, p


def _scores(k, q):
    return _dot(k, q).astype(BF16)


def _attn_kernel(qt_ref, k_ref, vt_ref, o_ref, acc_scr, m_scr, s_scr, *, heads, tq, n_chunks):
    ch = KV_CHUNK
    n_strips = heads * tq // Q_STRIP

    def q_strip(g):
        h, off = divmod(g * Q_STRIP, tq)
        return qt_ref[0, h, :, off:off + Q_STRIP]

    def keys(c):
        return k_ref[0, 0, pl.ds(pl.multiple_of(c * ch, ch), ch), :]

    m_scr[...] = jnp.full(m_scr.shape, -1e30, F32)
    acc_scr[...] = jnp.zeros(acc_scr.shape, F32)

    def chunk(c, issue_next):
        k = keys(c)
        vt = vt_ref[0, 0, :, pl.ds(pl.multiple_of(c * ch, ch), ch)]
        m_all = m_scr[...]
        scores = [s_scr[...]] + [None] * (n_strips - 1)
        probs = [None] * n_strips
        for t in range(n_strips + 1):
            if t + 1 < n_strips:
                scores[t + 1] = _scores(k, q_strip(t + 1))
            elif t + 1 == n_strips and issue_next:
                s_scr[...] = _scores(keys(c + 1), q_strip(0))
            if t < n_strips:
                cols = slice(t * Q_STRIP, (t + 1) * Q_STRIP)
                m_new, alpha, p = _softmax_strip(scores[t], m_all[:, cols])
                probs[t] = (m_new, alpha, p)
            if t >= 1:
                cols = slice((t - 1) * Q_STRIP, t * Q_STRIP)
                m_new, alpha, p = probs[t - 1]
                m_scr[:, cols] = m_new
                acc_scr[:, cols] = alpha * acc_scr[:, cols] + _dot(vt, p)

    s_scr[...] = _scores(keys(0), q_strip(0))

    def body(c, carry):
        chunk(c, True)
        return carry

    lax.fori_loop(0, n_chunks - 1, body, 0)
    chunk(n_chunks - 1, False)

    inv_l = 1.0 / acc_scr[GQA_HD:GQA_HD + 1, :]
    for g in range(heads):
        for j in range(tq // Q_STRIP):
            cols = slice(g * tq + j * Q_STRIP, g * tq + (j + 1) * Q_STRIP)
            o = acc_scr[0:GQA_HD, cols] * inv_l[:, cols]
            o_ref[0, j * Q_STRIP:(j + 1) * Q_STRIP, g * GQA_HD:(g + 1) * GQA_HD] = o.T.astype(o_ref.dtype)


def _ctx_attn_kernel(qt_ref, k_ref, vt_ref, o_in_ref, o_ref, *, heads):
    del o_in_ref
    q = jnp.concatenate([qt_ref[0, g] for g in range(heads)], axis=1) if heads > 1 else qt_ref[0, 0]
    s = _scores(k_ref[0, 0], q)
    _, _, p = _softmax_strip(s, jnp.full((1, s.shape[1]), -1e30, F32))
    acc = _dot(vt_ref[0, 0], p)
    o = acc[0:GQA_HD] / acc[GQA_HD:GQA_HD + 1]
    n = o.shape[1] // heads
    for g in range(heads):
        o_ref[0, :, g * GQA_HD:(g + 1) * GQA_HD] = o[:, g * n:(g + 1) * n].T.astype(o_ref.dtype)


def _attention(qt, k, vt, *, heads, tq, s_lat, s_tot, name):
    batch, n_heads, dk, _ = qt.shape
    n_kv = k.shape[1]
    n_ctx = s_tot - s_lat
    tq = min(tq, s_lat)
    assert s_tot % KV_CHUNK == 0 and s_lat % tq == 0 and tq % Q_STRIP == 0 and s_lat % n_ctx == 0
    kern = functools.partial(_attn_kernel, heads=heads, tq=tq, n_chunks=s_tot // KV_CHUNK)
    out_shape = jax.ShapeDtypeStruct((batch, s_tot, n_heads * GQA_HD), BF16)
    o = pl.pallas_call(
        kern,
        grid=(batch, n_kv, s_lat // tq),
        in_specs=[
            pl.BlockSpec((1, heads, dk, tq), lambda b, n, q: (b, n, 0, q)),
            pl.BlockSpec((1, 1, s_tot, dk), lambda b, n, q: (b, n, 0, 0)),
            pl.BlockSpec((1, 1, VT_ROWS, s_tot), lambda b, n, q: (b, n, 0, 0)),
        ],
        out_specs=pl.BlockSpec((1, tq, heads * GQA_HD), lambda b, n, q: (b, q, n)),
        out_shape=out_shape,
        scratch_shapes=[
            pltpu.VMEM((VT_ROWS, heads * tq), F32),
            pltpu.VMEM((1, heads * tq), F32),
            pltpu.VMEM((KV_CHUNK, Q_STRIP), BF16),
        ],
        compiler_params=_params(("arbitrary", "arbitrary", "arbitrary")),
        name=name,
    )(qt, k, vt)
    ctx_blk = s_lat // n_ctx
    return pl.pallas_call(
        functools.partial(_ctx_attn_kernel, heads=heads),
        grid=(batch, n_kv),
        in_specs=[
            pl.BlockSpec((1, heads, dk, n_ctx), lambda b, n: (b, n, 0, ctx_blk)),
            pl.BlockSpec((1, 1, n_ctx, dk), lambda b, n: (b, n, ctx_blk, 0)),
            pl.BlockSpec((1, 1, VT_ROWS, n_ctx), lambda b, n: (b, n, 0, ctx_blk)),
            pl.BlockSpec(memory_space=pl.ANY),
        ],
        out_specs=pl.BlockSpec((1, n_ctx, heads * GQA_HD), lambda b, n: (b, ctx_blk, n)),
        out_shape=out_shape,
        input_output_aliases={3: 0},
        compiler_params=_params(("arbitrary", "arbitrary")),
        name=name + "_ctx",
    )(qt, k, vt, o)


def _shift_rows(x, s, fill, reverse):
    n = x.shape[0]
    if s % SUBLANES == 0:
        pad = jnp.full((s, x.shape[1]), fill, x.dtype)
        return jnp.concatenate([x[s:], pad] if reverse else [pad, x[:n - s]], axis=0)
    t = lax.broadcasted_iota(jnp.int32, x.shape, 0)
    if reverse:
        return jnp.where(t < n - s, pltpu.roll(x, n - s, axis=0), fill)
    return jnp.where(t >= s, pltpu.roll(x, s, axis=0), fill)


def _tile_scan(a, b, h0, reverse):
    s = 1
    while s < a.shape[0]:
        b = a * _shift_rows(b, s, 0.0, reverse) + b
        a = a * _shift_rows(a, s, 1.0, reverse)
        s *= 2
    return a * h0 + b


def _gelu_tanh(x):
    return x * jax.nn.sigmoid(2.0 * math.sqrt(2.0 / math.pi) * (x + 0.044715 * x * x * x))


def _rglru_kernel(rx_ref, ry_ref, cw_ref, cb_ref, wa_ref, ba_ref, wx_ref, bx_ref, lam_ref,
                  o_ref, x_scr, hf_scr, *, s_lat, n_ctx):
    tt = SCAN_TILE
    x_scr[...] = rx_ref[...].astype(F32)

    cw = cw_ref[...]
    cb = cb_ref[...]
    lam = -lam_ref[...]
    softplus = jnp.log1p(jnp.exp(-jnp.abs(lam))) + jnp.maximum(lam, 0.0)
    zero_rows = jnp.zeros((SUBLANES, RNN_BW), F32)

    def conv_tile(row0, n, prev_tail, next_head):
        cur = x_scr[pl.ds(row0, n), :]
        ext = jnp.concatenate([prev_tail, cur, next_head], axis=0)
        xc = cb
        for j in range(CONV_W):
            o = SUBLANES - CONV_W // 2 + j
            xc = xc + ext[o:o + n, :] * cw[j:j + 1, :]
        x_scr[pl.ds(row0, n), :] = xc
        return xc, cur[n - SUBLANES:n, :]

    def coeffs(xc, d):
        xb = xc.astype(BF16)
        r = jax.nn.sigmoid(_dot(xb, wa_ref[d, 0]) + ba_ref[d:d + 1, :])
        i = jax.nn.sigmoid(_dot(xb, wx_ref[d, 0]) + bx_ref[d:d + 1, :])
        log_a = -LRU_C * r * softplus[d:d + 1, :]
        a = jnp.exp(log_a)
        y = jnp.maximum(jnp.tanh(-log_a) * (1.0 + a * a), 0.0)
        mult = jnp.where(y > 0.0, y * lax.rsqrt(y), 0.0)
        return a, mult * i * xc

    def fwd_tile(row0, n, xc, h0):
        a, b = coeffs(xc, 0)
        h = _tile_scan(a, b, h0, reverse=False)
        hf_scr[pl.ds(row0, n), :] = h
        return h[n - 1:n, :]

    def bwd_tile(row0, n, h0):
        a, b = coeffs(x_scr[pl.ds(row0, n), :], 1)
        h = _tile_scan(a, b, h0, reverse=True)
        y = ry_ref[pl.ds(row0, n), :].astype(F32)
        o_ref[pl.ds(row0, n), :] = ((hf_scr[pl.ds(row0, n), :] + h) * _gelu_tanh(y)).astype(o_ref.dtype)
        return h[0:1, :]

    n_lat_tiles = s_lat // tt
    h0 = jnp.zeros((1, RNN_BW), F32)

    xc, _ = conv_tile(s_lat, n_ctx, zero_rows, zero_rows)
    hc = fwd_tile(s_lat, n_ctx, xc, h0)

    def fwd(t, carry):
        h, tail = carry
        r0 = pl.multiple_of(t * tt, tt)
        nxt = jnp.where(t == n_lat_tiles - 1, 0.0, x_scr[pl.ds(r0 + tt, SUBLANES), :])
        xc, tail = conv_tile(r0, tt, tail, nxt)
        return fwd_tile(r0, tt, xc, h), tail

    lax.fori_loop(0, n_lat_tiles, fwd, (hc, zero_rows))

    hc = bwd_tile(s_lat, n_ctx, h0)

    def bwd(t, h):
        r0 = pl.multiple_of((n_lat_tiles - 1 - t) * tt, tt)
        return bwd_tile(r0, tt, h)

    lax.fori_loop(0, n_lat_tiles, bwd, hc)


def _rglru(z, lp, *, s_lat, s_tot, batch):
    r = z.shape[0]
    n_ctx = s_tot - s_lat
    bw = RNN_BW
    kern = functools.partial(_rglru_kernel, s_lat=s_lat, n_ctx=n_ctx)
    vec = lambda b, k: (0, k)
    return pl.pallas_call(
        kern,
        grid=(batch, RNN_BLOCKS),
        in_specs=[
            pl.BlockSpec((s_tot, bw), lambda b, k: (b, C_RX // bw + k)),
            pl.BlockSpec((s_tot, bw), lambda b, k: (b, C_RY // bw + k)),
            pl.BlockSpec((CONV_W, bw), vec),
            pl.BlockSpec((1, bw), vec),
            pl.BlockSpec((2, 1, bw, bw), lambda b, k: (0, k, 0, 0)),
            pl.BlockSpec((2, bw), vec),
            pl.BlockSpec((2, 1, bw, bw), lambda b, k: (0, k, 0, 0)),
            pl.BlockSpec((2, bw), vec),
            pl.BlockSpec((2, bw), vec),
        ],
        out_specs=pl.BlockSpec((s_tot, bw), lambda b, k: (b, k)),
        out_shape=jax.ShapeDtypeStruct((r, D_MODEL), BF16),
        scratch_shapes=[
            pltpu.VMEM((s_tot, bw), F32),
            pltpu.VMEM((s_tot, bw), F32),
        ],
        compiler_params=_params(("arbitrary", "arbitrary")),
        name="rglru",
    )(z, z, lp["rg_conv_w"], lp["rg_conv_b"], lp["rg_a_w"], lp["rg_a_b"], lp["rg_x_w"], lp["rg_x_b"],
      lp["rg_lambda"])


def _merge_kernel(om_ref, og_ref, rg_ref, gm_ref, gg_ref, gr_ref, wm_ref, wg_ref, wr_ref, y_ref):
    y = (jax.nn.sigmoid(gm_ref[...].astype(F32)) * _dot(om_ref[...], wm_ref[...])
         + jax.nn.sigmoid(gg_ref[...].astype(F32)) * _dot(og_ref[...], wg_ref[...])
         + jax.nn.sigmoid(gr_ref[...].astype(F32)) * _dot(rg_ref[...], wr_ref[...]))
    y_ref[...] = y.astype(y_ref.dtype)


def _merge(o_mla, o_gqa, rg, z, lp):
    r, d = o_mla.shape
    tm, tn = ROW_TILE, 512
    gate0 = C_GATE // tn
    per = d // tn
    act = pl.BlockSpec((tm, d), lambda i, j: (i, 0))
    wgt = pl.BlockSpec((d, tn), lambda i, j: (0, j))
    return pl.pallas_call(
        _merge_kernel,
        grid=(r // tm, d // tn),
        in_specs=[
            act, act, act,
            pl.BlockSpec((tm, tn), lambda i, j: (i, gate0 + j)),
            pl.BlockSpec((tm, tn), lambda i, j: (i, gate0 + per + j)),
            pl.BlockSpec((tm, tn), lambda i, j: (i, gate0 + 2 * per + j)),
            wgt, wgt, wgt,
        ],
        out_specs=pl.BlockSpec((tm, tn), lambda i, j: (i, j)),
        out_shape=jax.ShapeDtypeStruct((r, d), BF16),
        compiler_params=_params(("arbitrary", "arbitrary")),
        name="merge_branches",
    )(o_mla, o_gqa, rg, z, z, z, lp["mla_out"], lp["gqa_out"], lp["rg_out"])


def _resid_kernel(y_ref, w_ref, h_ref, ml_ref, mc_ref, o_ref, *, tm, tpb, s_lat, tn):
    i, j = pl.program_id(0), pl.program_id(1)
    is_ctx = _is_ctx_rows(i, tpb, tm, s_lat)
    col = pl.ds(pl.multiple_of(j * tn, tn), tn)
    gate = jnp.where(is_ctx, mc_ref[0, 2:3, col], ml_ref[0, 2:3, col])
    o_ref[...] = h_ref[...] + gate * _dot(y_ref[...], w_ref[...])


def _merge_out(y, w, h, mods, *, s_lat, s_tot):
    r, d = h.shape
    tm, tn = ROW_TILE, 512
    tpb = s_tot // tm
    return pl.pallas_call(
        functools.partial(_resid_kernel, tm=tm, tpb=tpb, s_lat=s_lat, tn=tn),
        grid=(r // tm, d // tn),
        in_specs=[
            pl.BlockSpec((tm, d), lambda i, j: (i, 0)),
            pl.BlockSpec((d, tn), lambda i, j: (0, j)),
            pl.BlockSpec((tm, tn), lambda i, j: (i, j)),
            pl.BlockSpec((1, 6, d), lambda i, j: (i // tpb, 0, 0)),
            pl.BlockSpec((1, 6, d), lambda i, j: (2, 0, 0)),
        ],
        out_specs=pl.BlockSpec((tm, tn), lambda i, j: (i, j)),
        out_shape=jax.ShapeDtypeStruct((r, d), F32),
        compiler_params=_params(("arbitrary", "arbitrary")),
        name="merge_out_residual",
    )(y, w, h, mods, mods)


def _ffn_kernel(h_ref, nw_ref, ml_ref, mc_ref, wg_ref, wu_ref, wd_ref, fw_ref, o_ref, u_scr,
                *, tm, tpb, s_lat, final_norm):
    i, j = pl.program_id(0), pl.program_id(1)
    o_ref = o_ref.at[0]

    @pl.when(j == 0)
    def _():
        _mod_norm_to(u_scr, h_ref, nw_ref, ml_ref, mc_ref, i, tpb, tm, s_lat, 3, 4)

    u = u_scr[...]
    g = _dot(u, wg_ref[...])
    g = g * jax.nn.sigmoid(g) * _dot(u, wu_ref[...])
    part = _dot(g.astype(BF16), wd_ref[...])

    @pl.when(j == 0)
    def _():
        o_ref[...] = part

    @pl.when(j > 0)
    def _():
        o_ref[...] += part

    @pl.when(j == pl.num_programs(1) - 1)
    def _():
        fw = fw_ref[...]
        for r0 in range(0, tm, ROW_CHUNK):
            rows = slice(r0, r0 + ROW_CHUNK)
            is_ctx = _is_ctx_rows(i, tpb, tm, s_lat, r0, ROW_CHUNK)
            gate = jnp.where(is_ctx, mc_ref[0, 5:6], ml_ref[0, 5:6])
            out = h_ref[rows, :] + gate * o_ref[rows, :]
            if final_norm:
                out = _rms(out, fw)
            o_ref[rows, :] = out


def _ffn(h, nw, mods, wg, wu, wd, fw, *, s_lat, s_tot, final_norm):
    r, d = h.shape
    out_rows = s_lat if final_norm else s_tot
    hid = wg.shape[1]
    tm, th = ROW_TILE, 512
    tpb = s_tot // tm
    kern = functools.partial(_ffn_kernel, tm=tm, tpb=tpb, s_lat=s_lat, final_norm=final_norm)
    return pl.pallas_call(
        kern,
        grid=(r // tm, hid // th),
        in_specs=[
            pl.BlockSpec((tm, d), lambda i, j: (i, 0)),
            pl.BlockSpec((1, d), lambda i, j: (0, 0)),
            pl.BlockSpec((1, 6, d), lambda i, j: (i // tpb, 0, 0)),
            pl.BlockSpec((1, 6, d), lambda i, j: (2, 0, 0)),
            pl.BlockSpec((d, th), lambda i, j: (0, j)),
            pl.BlockSpec((d, th), lambda i, j: (0, j)),
            pl.BlockSpec((th, d), lambda i, j: (j, 0)),
            pl.BlockSpec((1, d), lambda i, j: (0, 0)),
        ],
        out_specs=pl.BlockSpec((1, tm, d), lambda i, j: (i // tpb, i % tpb, 0)),
        out_shape=jax.ShapeDtypeStruct((r // s_tot, out_rows, d), F32),
        scratch_shapes=[pltpu.VMEM((tm, d), BF16)],
        compiler_params=_params(("arbitrary", "arbitrary")),
        name="ffn",
    )(h, nw, mods, mods, wg, wu, wd, fw)


def _rope_tables(s_lat, n_ctx, dim):
    quarter = dim // 4
    inv = ROPE_THETA ** (-jnp.arange(quarter, dtype=F32) / quarter)
    t = jnp.arange(s_lat)
    ang_r = (t // GRID_W).astype(F32)[:, None] * inv
    ang_c = (t % GRID_W).astype(F32)[:, None] * inv
    cos = jnp.concatenate([jnp.cos(ang_r)] * 2 + [jnp.cos(ang_c)] * 2, axis=1)
    sin = jnp.concatenate([-jnp.sin(ang_r), jnp.sin(ang_r), -jnp.sin(ang_c), jnp.sin(ang_c)], axis=1)
    cos = jnp.concatenate([cos, jnp.ones((n_ctx, dim), F32)], axis=0)
    sin = jnp.concatenate([sin, jnp.zeros((n_ctx, dim), F32)], axis=0)
    padw = LANE - dim
    if padw:
        cos = jnp.pad(cos, ((0, 0), (0, padw)))
        sin = jnp.pad(sin, ((0, 0), (0, padw)))
    return cos, sin


def _pad_w_in(w):
    d = w.shape[0]
    z = lambda n: jnp.zeros((d, n), w.dtype)
    ckv_kr = w[:, 0:576]
    gk_gv = w[:, 576:1600]
    rx = w[:, 1600:3648]
    cq = w[:, 3648:4160]
    gq = w[:, 4160:6208]
    ry = w[:, 6208:8256]
    gate = w[:, 8256:14400]
    return jnp.concatenate([ckv_kr, z(64), gk_gv, z(384), rx, gq, ry, gate, cq], axis=1).astype(BF16)


def _pad_q_up(w):
    rank = w.shape[0]
    w = w.reshape(rank, MLA_HEADS, MLA_NOPE + MLA_ROPE)
    w = jnp.pad(w, ((0, 0), (0, 0), (0, MLA_QK_PAD - MLA_NOPE - MLA_ROPE)))
    return w.reshape(rank, MLA_HEADS * MLA_QK_PAD).astype(BF16)


def kernel(x, c, ctx, c_ctx, mod_w, mod_b, norm_mix_w, norm_ffn_w, w_in, mla_q_norm_w, mla_q_up,
           mla_kv_norm_w, mla_kv_up, mla_out, gqa_q_norm_w, gqa_k_norm_w, gqa_out, rg_conv_w, rg_conv_b,
           rg_a_w, rg_a_b, rg_x_w, rg_x_b, rg_lambda, rg_out, merge_out, ffn_w_gate, ffn_w_up, ffn_w_down,
           final_norm_w):
    batch, s_lat, d = x.shape
    n_ctx = ctx.shape[1]
    depth = mod_w.shape[0]
    s_tot = s_lat + n_ctx
    assert batch == 2 and d == D_MODEL and n_ctx == SCAN_TILE
    assert s_tot % ROW_TILE == 0 and s_tot % PREP_TILE == 0 and s_lat % SCAN_TILE == 0 and s_lat % GRID_W == 0

    h = jnp.concatenate([x, ctx], axis=1).reshape(batch * s_tot, d)
    cs = jnp.zeros((8, d), F32).at[0:batch].set(c).at[batch].set(c_ctx)
    mods_all = _mods(cs, mod_w, mod_b).reshape(depth, 8, 6, d)
    tabs = _rope_tables(s_lat, n_ctx, MLA_ROPE) + _rope_tables(s_lat, n_ctx, GQA_HD)
    row2 = lambda v: v.reshape(1, -1)
    dims = dict(s_lat=s_lat, s_tot=s_tot)

    for l in range(depth):
        last = l == depth - 1
        mods = mods_all[l]
        lp = {
            "mla_kv_norm_w": row2(mla_kv_norm_w[l]), "mla_q_norm_w": row2(mla_q_norm_w[l]),
            "mla_kv_up": mla_kv_up[l].astype(BF16), "mla_q_up": _pad_q_up(mla_q_up[l]),
            "gqa_k_norm_w": row2(gqa_k_norm_w[l]), "gqa_q_norm_w": row2(gqa_q_norm_w[l]),
            "rg_conv_w": rg_conv_w[l], "rg_conv_b": row2(rg_conv_b[l]),
            "rg_a_w": rg_a_w[l].astype(BF16), "rg_a_b": rg_a_b[l],
            "rg_x_w": rg_x_w[l].astype(BF16), "rg_x_b": rg_x_b[l], "rg_lambda": rg_lambda[l],
            "mla_out": mla_out[l].astype(BF16), "gqa_out": gqa_out[l].astype(BF16),
            "rg_out": rg_out[l].astype(BF16),
        }
        z = _inproj(h, row2(norm_mix_w[l]), mods, _pad_w_in(w_in[l]), **dims)
        kmla, vtm, qtm, kg, vtg, qtg = _prep(z, lp, tabs, s_tot=s_tot, batch=batch)
        o_mla = _attention(qtm, kmla, vtm, heads=1, tq=MLA_Q_TILE, name="mla_attention", **dims)
        o_gqa = _attention(qtg, kg, vtg, heads=GQA_GROUP, tq=GQA_Q_TILE, name="gqa_attention", **dims)
        o_mla = o_mla.reshape(batch * s_tot, d)
        o_gqa = o_gqa.reshape(batch * s_tot, d)
        rg = _rglru(z, lp, batch=batch, **dims)
        y = _merge(o_mla, o_gqa, rg, z, lp)
        h = _merge_out(y, merge_out[l].astype(BF16), h, mods, **dims)
        h = _ffn(h, row2(norm_ffn_w[l]), mods, ffn_w_gate[l].astype(BF16), ffn_w_up[l].astype(BF16),
                 ffn_w_down[l].astype(BF16), row2(final_norm_w), final_norm=last, **dims)
        if not last:
            h = h.reshape(batch * s_tot, d)

    return h
```

```python
import functools
import math

import jax
import jax.numpy as jnp
from jax import lax
from jax.experimental import pallas as pl
from jax.experimental.pallas import tpu as pltpu

F32 = jnp.float32
BF16 = jnp.bfloat16

D_MODEL = 2048
GRID_W = 64
ROPE_THETA = 10000.0
NORM_EPS = 1e-6

MLA_HEADS = 16
MLA_RANK = 512
MLA_NOPE = 128
MLA_ROPE = 64
MLA_V = 128
MLA_QK_PAD = 256

GQA_HEADS = 16
GQA_KV = 4
GQA_GROUP = GQA_HEADS // GQA_KV
GQA_HD = 128

RNN_BLOCKS = 8
RNN_BW = D_MODEL // RNN_BLOCKS
CONV_W = 4
LRU_C = 8.0
FFN_HIDDEN = 5632

C_CKV = 0
C_KR = 512
C_GK = 640
C_GV = 1152
C_RX = 2048
C_GQ = 4096
C_RY = 6144
C_GATE = 8192
C_CQ = 14336
Z_COLS = 14848

LANE = 128
SUBLANES = 8
VMEM_CAP = 64 * 1024 * 1024
VMEM_LIMIT = 56 * 1024 * 1024

ROW_TILE = 768
ROW_CHUNK = 128
PREP_TILE = 384
Q_STRIP = 256
GQA_Q_TILE = 1024
MLA_Q_TILE = 4096
KV_CHUNKS = (1408, 768)
INPROJ_TILES = (1408, 768)
VT_ROWS = 144
SCAN_TILE = 256
LOG2E = 1.4426950408889634


def _params(sem):
    return pltpu.CompilerParams(dimension_semantics=sem, vmem_limit_bytes=VMEM_LIMIT)


def _dot(a, b):
    return jnp.dot(a, b, preferred_element_type=F32)


def _first_divisor(candidates, n):
    return next(c for c in candidates if n % c == 0)


def _rms(x, w):
    return x * lax.rsqrt(jnp.mean(x * x, axis=-1, keepdims=True) + NORM_EPS) * w


def _is_ctx_rows(tile_idx, tiles_per_batch, tm, s_lat, r0=0, n=None):
    n = tm if n is None else n
    r = (tile_idx % tiles_per_batch) * tm + r0 + lax.broadcasted_iota(jnp.int32, (n, 1), 0)
    return r >= s_lat


def _mod_norm_to(u_scr, h_ref, nw_ref, ml_ref, mc_ref, tile_idx, tpb, tm, s_lat, k_shift, k_scale):
    nw = nw_ref[...]
    for r0 in range(0, tm, ROW_CHUNK):
        is_ctx = _is_ctx_rows(tile_idx, tpb, tm, s_lat, r0, ROW_CHUNK)
        y = _rms(h_ref[r0:r0 + ROW_CHUNK, :], nw)
        shift = jnp.where(is_ctx, mc_ref[0, k_shift:k_shift + 1], ml_ref[0, k_shift:k_shift + 1])
        scale = jnp.where(is_ctx, mc_ref[0, k_scale:k_scale + 1], ml_ref[0, k_scale:k_scale + 1])
        u_scr[r0:r0 + ROW_CHUNK, :] = (y * (1.0 + scale) + shift).astype(BF16)


def _rope(x, cos, sin_signed, half):
    n = x.shape[-1]
    lane = lax.broadcasted_iota(jnp.int32, x.shape, 1)
    up = pltpu.roll(x, n - half, axis=1)
    dn = pltpu.roll(x, half, axis=1)
    partner = jnp.where((lane % (2 * half)) < half, up, dn)
    return x * cos + partner * sin_signed


def _mods_kernel(c_ref, w_ref, b_ref, o_ref):
    x = c_ref[...]
    x = x * jax.nn.sigmoid(x)
    w = w_ref[0]
    xh = x.astype(BF16)
    xl = (x - xh.astype(F32)).astype(BF16)
    wh = w.astype(BF16)
    wl = (w - wh.astype(F32)).astype(BF16)
    o_ref[0] = _dot(xh, wh) + _dot(xl, wh) + _dot(xh, wl) + b_ref[0]


def _mods(cs, mod_w, mod_b):
    depth, d, n = mod_w.shape
    tn = 512
    return pl.pallas_call(
        _mods_kernel,
        grid=(depth, n // tn),
        in_specs=[
            pl.BlockSpec((8, d), lambda l, j: (0, 0)),
            pl.BlockSpec((1, d, tn), lambda l, j: (l, 0, j)),
            pl.BlockSpec((1, 1, tn), lambda l, j: (l, 0, j)),
        ],
        out_specs=pl.BlockSpec((1, 8, tn), lambda l, j: (l, 0, j)),
        out_shape=jax.ShapeDtypeStruct((depth, 8, n), F32),
        compiler_params=_params(("arbitrary", "arbitrary")),
        name="adaln_mods",
    )(cs, mod_w, mod_b.reshape(depth, 1, n))


def _inproj_kernel(h_ref, nw_ref, ml_ref, mc_ref, w_ref, z_ref, u_scr, *, tm, tpb, s_lat):
    i = pl.program_id(0)

    @pl.when(pl.program_id(1) == 0)
    def _():
        _mod_norm_to(u_scr, h_ref, nw_ref, ml_ref, mc_ref, i, tpb, tm, s_lat, 0, 1)

    z_ref[...] = _dot(u_scr[...], w_ref[...]).astype(z_ref.dtype)


def _inproj(h, nw, mods, w_pad, *, s_lat, s_tot):
    r, d = h.shape
    tm, tn = _first_divisor(INPROJ_TILES, s_tot), 512
    tpb = s_tot // tm
    return pl.pallas_call(
        functools.partial(_inproj_kernel, tm=tm, tpb=tpb, s_lat=s_lat),
        grid=(r // tm, Z_COLS // tn),
        in_specs=[
            pl.BlockSpec((tm, d), lambda i, j: (i, 0)),
            pl.BlockSpec((1, d), lambda i, j: (0, 0)),
            pl.BlockSpec((1, 6, d), lambda i, j: (i // tpb, 0, 0)),
            pl.BlockSpec((1, 6, d), lambda i, j: (2, 0, 0)),
            pl.BlockSpec((d, tn), lambda i, j: (0, j)),
        ],
        out_specs=pl.BlockSpec((tm, tn), lambda i, j: (i, j)),
        out_shape=jax.ShapeDtypeStruct((r, Z_COLS), BF16),
        scratch_shapes=[pltpu.VMEM((tm, d), BF16)],
        compiler_params=_params(("arbitrary", "arbitrary")),
        name="in_proj",
    )(h, nw, mods, mods, w_pad)


def _prep_kernel(za_ref, cq_ref, gq_ref, kvw_ref, qw_ref, kvup_ref, qup_ref, gkw_ref, gqw_ref,
                 cm_ref, sm_ref, cg_ref, sg_ref,
                 kmla_ref, vtm_ref, qtm_ref, kg_ref, vtg_ref, qtg_ref):
    cos_m, sin_m = cm_ref[...], sm_ref[...]
    cos_g, sin_g = cg_ref[...], sg_ref[...]
    mla_scale = (MLA_NOPE + MLA_ROPE) ** -0.5 * LOG2E
    gqa_scale = GQA_HD ** -0.5 * LOG2E

    ckv = _rms(za_ref[:, C_CKV:C_CKV + MLA_RANK].astype(F32), kvw_ref[...]).astype(BF16)
    kr = _rope(za_ref[:, C_KR:C_KR + LANE].astype(F32), cos_m, sin_m, MLA_ROPE // 4).astype(BF16)
    ones = jnp.ones((VT_ROWS - GQA_HD, za_ref.shape[0]), BF16)
    for h in range(MLA_HEADS):
        kv = _dot(ckv, kvup_ref[:, h * 256:(h + 1) * 256])
        kmla_ref[0, h, :, 0:MLA_NOPE] = kv[:, 0:MLA_NOPE].astype(BF16)
        kmla_ref[0, h, :, MLA_NOPE:MLA_QK_PAD] = kr
        vtm_ref[0, h, 0:MLA_V, :] = kv[:, MLA_NOPE:].T.astype(BF16)
        vtm_ref[0, h, MLA_V:VT_ROWS, :] = ones

    cq = _rms(cq_ref[...].astype(F32), qw_ref[...]).astype(BF16)
    for h in range(MLA_HEADS):
        q = _dot(cq, qup_ref[:, h * MLA_QK_PAD:(h + 1) * MLA_QK_PAD])
        q_rope = _rope(q[:, MLA_NOPE:], cos_m, sin_m, MLA_ROPE // 4)
        qtm_ref[0, h, 0:MLA_NOPE, :] = (q[:, 0:MLA_NOPE] * mla_scale).T.astype(BF16)
        qtm_ref[0, h, MLA_NOPE:MLA_QK_PAD, :] = (q_rope * mla_scale).T.astype(BF16)

    for n in range(GQA_KV):
        k = _rms(za_ref[:, C_GK + n * GQA_HD:C_GK + (n + 1) * GQA_HD].astype(F32), gkw_ref[...])
        kg_ref[0, n] = _rope(k, cos_g, sin_g, GQA_HD // 4).astype(BF16)
        v = za_ref[:, C_GV + n * GQA_HD:C_GV + (n + 1) * GQA_HD]
        vtg_ref[0, n, 0:GQA_HD, :] = v.astype(F32).T.astype(BF16)
        vtg_ref[0, n, GQA_HD:VT_ROWS, :] = ones
    for h in range(GQA_HEADS):
        q = _rms(gq_ref[:, h * GQA_HD:(h + 1) * GQA_HD].astype(F32), gqw_ref[...])
        q = _rope(q, cos_g, sin_g, GQA_HD // 4) * gqa_scale
        qtg_ref[0, h] = q.T.astype(BF16)


def _prep(z, lp, tabs, *, s_tot, batch):
    tm = PREP_TILE
    tpb = s_tot // tm
    cos_m, sin_m, cos_g, sin_g = tabs
    tab = lambda b, i: (i, 0)
    const = lambda b, i: (0, 0)
    rows_out = lambda b, i: (b, 0, i, 0)
    cols_out = lambda b, i: (b, 0, 0, i)
    outs = [
        jax.ShapeDtypeStruct((batch, MLA_HEADS, s_tot, MLA_QK_PAD), BF16),
        jax.ShapeDtypeStruct((batch, MLA_HEADS, VT_ROWS, s_tot), BF16),
        jax.ShapeDtypeStruct((batch, MLA_HEADS, MLA_QK_PAD, s_tot), BF16),
        jax.ShapeDtypeStruct((batch, GQA_KV, s_tot, GQA_HD), BF16),
        jax.ShapeDtypeStruct((batch, GQA_KV, VT_ROWS, s_tot), BF16),
        jax.ShapeDtypeStruct((batch, GQA_HEADS, GQA_HD, s_tot), BF16),
    ]
    return pl.pallas_call(
        _prep_kernel,
        grid=(batch, tpb),
        in_specs=[
            pl.BlockSpec((tm, 2048), lambda b, i: (b * tpb + i, 0)),
            pl.BlockSpec((tm, MLA_RANK), lambda b, i: (b * tpb + i, C_CQ // MLA_RANK)),
            pl.BlockSpec((tm, 2048), lambda b, i: (b * tpb + i, C_GQ // 2048)),
            pl.BlockSpec((1, MLA_RANK), const),
            pl.BlockSpec((1, MLA_RANK), const),
            pl.BlockSpec((MLA_RANK, MLA_HEADS * 256), const),
            pl.BlockSpec((MLA_RANK, MLA_HEADS * MLA_QK_PAD), const),
            pl.BlockSpec((1, GQA_HD), const),
            pl.BlockSpec((1, GQA_HD), const),
            pl.BlockSpec((tm, LANE), tab),
            pl.BlockSpec((tm, LANE), tab),
            pl.BlockSpec((tm, LANE), tab),
            pl.BlockSpec((tm, LANE), tab),
        ],
        out_specs=[
            pl.BlockSpec((1, MLA_HEADS, tm, MLA_QK_PAD), rows_out),
            pl.BlockSpec((1, MLA_HEADS, VT_ROWS, tm), cols_out),
            pl.BlockSpec((1, MLA_HEADS, MLA_QK_PAD, tm), cols_out),
            pl.BlockSpec((1, GQA_KV, tm, GQA_HD), rows_out),
            pl.BlockSpec((1, GQA_KV, VT_ROWS, tm), cols_out),
            pl.BlockSpec((1, GQA_HEADS, GQA_HD, tm), cols_out),
        ],
        out_shape=outs,
        compiler_params=_params(("arbitrary", "arbitrary")),
        name="attn_prep",
    )(z, z, z, lp["mla_kv_norm_w"], lp["mla_q_norm_w"], lp["mla_kv_up"], lp["mla_q_up"],
      lp["gqa_k_norm_w"], lp["gqa_q_norm_w"], cos_m, sin_m, cos_g, sin_g)


def _softmax_strip(s, m_old):
    m_new = jnp.maximum(m_old, jnp.max(s, axis=0, keepdims=True).astype(F32))
    alpha = jnp.exp2(m_old - m_new)
    p = jnp.exp2(s - m_new.astype(BF16))
    return m_new, alpha, p


def _scores(k, q):
    return _dot(k, q).astype(BF16)


def _attn_kernel(qt_ref, k_ref, vt_ref, o_ref, acc_scr, m_scr, s_scr, *, heads, tq, ch, n_chunks):
    n_strips = heads * tq // Q_STRIP

    def q_strip(g):
        h, off = divmod(g * Q_STRIP, tq)
        return qt_ref[0, h, :, off:off + Q_STRIP]

    def keys(c):
        return k_ref[0, 0, pl.ds(pl.multiple_of(c * ch, ch), ch), :]

    m_scr[...] = jnp.full(m_scr.shape, -1e30, F32)
    acc_scr[...] = jnp.zeros(acc_scr.shape, F32)

    def chunk(c, issue_next):
        k = keys(c)
        vt = vt_ref[0, 0, :, pl.ds(pl.multiple_of(c * ch, ch), ch)]
        m_all = m_scr[...]
        scores = [s_scr[...]] + [None] * (n_strips - 1)
        probs = [None] * n_strips
        for t in range(n_strips + 1):
            if t + 1 < n_strips:
                scores[t + 1] = _scores(k, q_strip(t + 1))
            elif t + 1 == n_strips and issue_next:
                s_scr[...] = _scores(keys(c + 1), q_strip(0))
            if t < n_strips:
                cols = slice(t * Q_STRIP, (t + 1) * Q_STRIP)
                m_new, alpha, p = _softmax_strip(scores[t], m_all[:, cols])
                probs[t] = (m_new, alpha, p)
            if t >= 1:
                cols = slice((t - 1) * Q_STRIP, t * Q_STRIP)
                m_new, alpha, p = probs[t - 1]
                m_scr[:, cols] = m_new
                acc_scr[:, cols] = alpha * acc_scr[:, cols] + _dot(vt, p)

    s_scr[...] = _scores(keys(0), q_strip(0))

    def body(c, carry):
        chunk(c, True)
        return carry

    lax.fori_loop(0, n_chunks - 1, body, 0)
    chunk(n_chunks - 1, False)

    inv_l = 1.0 / acc_scr[GQA_HD:GQA_HD + 1, :]
    for g in range(heads):
        for j in range(tq // Q_STRIP):
            cols = slice(g * tq + j * Q_STRIP, g * tq + (j + 1) * Q_STRIP)
            o = acc_scr[0:GQA_HD, cols] * inv_l[:, cols]
            o_ref[0, j * Q_STRIP:(j + 1) * Q_STRIP, g * GQA_HD:(g + 1) * GQA_HD] = o.T.astype(o_ref.dtype)


def _ctx_attn_kernel(qt_ref, k_ref, vt_ref, o_in_ref, o_ref, *, heads):
    del o_in_ref
    q = jnp.concatenate([qt_ref[0, g] for g in range(heads)], axis=1) if heads > 1 else qt_ref[0, 0]
    s = _scores(k_ref[0, 0], q)
    _, _, p = _softmax_strip(s, jnp.full((1, s.shape[1]), -1e30, F32))
    acc = _dot(vt_ref[0, 0], p)
    o = acc[0:GQA_HD] / acc[GQA_HD:GQA_HD + 1]
    n = o.shape[1] // heads
    for g in range(heads):
        o_ref[0, :, g * GQA_HD:(g + 1) * GQA_HD] = o[:, g * n:(g + 1) * n].T.astype(o_ref.dtype)


def _attention(qt, k, vt, *, heads, tq, s_lat, s_tot, name):
    batch, n_heads, dk, _ = qt.shape
    n_kv = k.shape[1]
    n_ctx = s_tot - s_lat
    tq = min(tq, s_lat)
    ch = _first_divisor(KV_CHUNKS, s_tot)
    assert s_lat % tq == 0 and tq % Q_STRIP == 0 and s_lat % n_ctx == 0
    kern = functools.partial(_attn_kernel, heads=heads, tq=tq, ch=ch, n_chunks=s_tot // ch)
    out_shape = jax.ShapeDtypeStruct((batch, s_tot, n_heads * GQA_HD), BF16)
    o = pl.pallas_call(
        kern,
        grid=(batch, n_kv, s_lat // tq),
        in_specs=[
            pl.BlockSpec((1, heads, dk, tq), lambda b, n, q: (b, n, 0, q)),
            pl.BlockSpec((1, 1, s_tot, dk), lambda b, n, q: (b, n, 0, 0)),
            pl.BlockSpec((1, 1, VT_ROWS, s_tot), lambda b, n, q: (b, n, 0, 0)),
        ],
        out_specs=pl.BlockSpec((1, tq, heads * GQA_HD), lambda b, n, q: (b, q, n)),
        out_shape=out_shape,
        scratch_shapes=[
            pltpu.VMEM((VT_ROWS, heads * tq), F32),
            pltpu.VMEM((1, heads * tq), F32),
            pltpu.VMEM((ch, Q_STRIP), BF16),
        ],
        compiler_params=_params(("arbitrary", "arbitrary", "arbitrary")),
        name=name,
    )(qt, k, vt)
    ctx_blk = s_lat // n_ctx
    return pl.pallas_call(
        functools.partial(_ctx_attn_kernel, heads=heads),
        grid=(batch, n_kv),
        in_specs=[
            pl.BlockSpec((1, heads, dk, n_ctx), lambda b, n: (b, n, 0, ctx_blk)),
            pl.BlockSpec((1, 1, n_ctx, dk), lambda b, n: (b, n, ctx_blk, 0)),
            pl.BlockSpec((1, 1, VT_ROWS, n_ctx), lambda b, n: (b, n, 0, ctx_blk)),
            pl.BlockSpec(memory_space=pl.ANY),
        ],
        out_specs=pl.BlockSpec((1, n_ctx, heads * GQA_HD), lambda b, n: (b, ctx_blk, n)),
        out_shape=out_shape,
        input_output_aliases={3: 0},
        compiler_params=_params(("arbitrary", "arbitrary")),
        name=name + "_ctx",
    )(qt, k, vt, o)


def _shift_rows(x, s, fill, reverse):
    n = x.shape[0]
    if s % SUBLANES == 0:
        pad = jnp.full((s, x.shape[1]), fill, x.dtype)
        return jnp.concatenate([x[s:], pad] if reverse else [pad, x[:n - s]], axis=0)
    t = lax.broadcasted_iota(jnp.int32, x.shape, 0)
    if reverse:
        return jnp.where(t < n - s, pltpu.roll(x, n - s, axis=0), fill)
    return jnp.where(t >= s, pltpu.roll(x, s, axis=0), fill)


def _tile_scan(a, b, h0, reverse):
    s = 1
    while s < a.shape[0]:
        b = a * _shift_rows(b, s, 0.0, reverse) + b
        a = a * _shift_rows(a, s, 1.0, reverse)
        s *= 2
    return a * h0 + b


def _gelu_tanh(x):
    return x * jax.nn.sigmoid(2.0 * math.sqrt(2.0 / math.pi) * (x + 0.044715 * x * x * x))


def _rglru_kernel(rx_ref, ry_ref, cw_ref, cb_ref, wa_ref, ba_ref, wx_ref, bx_ref, lam_ref,
                  o_ref, x_scr, hf_scr, *, s_lat, n_ctx):
    tt = SCAN_TILE
    x_scr[...] = rx_ref[...].astype(F32)

    cw = cw_ref[...]
    cb = cb_ref[...]
    lam = -lam_ref[...]
    softplus = jnp.log1p(jnp.exp(-jnp.abs(lam))) + jnp.maximum(lam, 0.0)
    zero_rows = jnp.zeros((SUBLANES, RNN_BW), F32)

    def conv_tile(row0, n, prev_tail, next_head):
        cur = x_scr[pl.ds(row0, n), :]
        ext = jnp.concatenate([prev_tail, cur, next_head], axis=0)
        xc = cb
        for j in range(CONV_W):
            o = SUBLANES - CONV_W // 2 + j
            xc = xc + ext[o:o + n, :] * cw[j:j + 1, :]
        x_scr[pl.ds(row0, n), :] = xc
        return xc, cur[n - SUBLANES:n, :]

    def coeffs(xc, d):
        xb = xc.astype(BF16)
        r = jax.nn.sigmoid(_dot(xb, wa_ref[d, 0]) + ba_ref[d:d + 1, :])
        i = jax.nn.sigmoid(_dot(xb, wx_ref[d, 0]) + bx_ref[d:d + 1, :])
        log_a = -LRU_C * r * softplus[d:d + 1, :]
        a = jnp.exp(log_a)
        y = jnp.maximum(jnp.tanh(-log_a) * (1.0 + a * a), 0.0)
        mult = jnp.where(y > 0.0, y * lax.rsqrt(y), 0.0)
        return a, mult * i * xc

    def fwd_tile(row0, n, xc, h0):
        a, b = coeffs(xc, 0)
        h = _tile_scan(a, b, h0, reverse=False)
        hf_scr[pl.ds(row0, n), :] = h
        return h[n - 1:n, :]

    def bwd_tile(row0, n, h0):
        a, b = coeffs(x_scr[pl.ds(row0, n), :], 1)
        h = _tile_scan(a, b, h0, reverse=True)
        y = ry_ref[pl.ds(row0, n), :].astype(F32)
        o_ref[pl.ds(row0, n), :] = ((hf_scr[pl.ds(row0, n), :] + h) * _gelu_tanh(y)).astype(o_ref.dtype)
        return h[0:1, :]

    n_lat_tiles = s_lat // tt
    h0 = jnp.zeros((1, RNN_BW), F32)

    xc, _ = conv_tile(s_lat, n_ctx, zero_rows, zero_rows)
    hc = fwd_tile(s_lat, n_ctx, xc, h0)

    def fwd(t, carry):
        h, tail = carry
        r0 = pl.multiple_of(t * tt, tt)
        nxt = jnp.where(t == n_lat_tiles - 1, 0.0, x_scr[pl.ds(r0 + tt, SUBLANES), :])
        xc, tail = conv_tile(r0, tt, tail, nxt)
        return fwd_tile(r0, tt, xc, h), tail

    lax.fori_loop(0, n_lat_tiles, fwd, (hc, zero_rows))

    hc = bwd_tile(s_lat, n_ctx, h0)

    def bwd(t, h):
        r0 = pl.multiple_of((n_lat_tiles - 1 - t) * tt, tt)
        return bwd_tile(r0, tt, h)

    lax.fori_loop(0, n_lat_tiles, bwd, hc)


def _rglru(z, lp, *, s_lat, s_tot, batch):
    r = z.shape[0]
    n_ctx = s_tot - s_lat
    bw = RNN_BW
    kern = functools.partial(_rglru_kernel, s_lat=s_lat, n_ctx=n_ctx)
    vec = lambda b, k: (0, k)
    return pl.pallas_call(
        kern,
        grid=(batch, RNN_BLOCKS),
        in_specs=[
            pl.BlockSpec((s_tot, bw), lambda b, k: (b, C_RX // bw + k)),
            pl.BlockSpec((s_tot, bw), lambda b, k: (b, C_RY // bw + k)),
            pl.BlockSpec((CONV_W, bw), vec),
            pl.BlockSpec((1, bw), vec),
            pl.BlockSpec((2, 1, bw, bw), lambda b, k: (0, k, 0, 0)),
            pl.BlockSpec((2, bw), vec),
            pl.BlockSpec((2, 1, bw, bw), lambda b, k: (0, k, 0, 0)),
            pl.BlockSpec((2, bw), vec),
            pl.BlockSpec((2, bw), vec),
        ],
        out_specs=pl.BlockSpec((s_tot, bw), lambda b, k: (b, k)),
        out_shape=jax.ShapeDtypeStruct((r, D_MODEL), BF16),
        scratch_shapes=[
            pltpu.VMEM((s_tot, bw), F32),
            pltpu.VMEM((s_tot, bw), F32),
        ],
        compiler_params=_params(("arbitrary", "arbitrary")),
        name="rglru",
    )(z, z, lp["rg_conv_w"], lp["rg_conv_b"], lp["rg_a_w"], lp["rg_a_b"], lp["rg_x_w"], lp["rg_x_b"],
      lp["rg_lambda"])


def _merge_kernel(om_ref, og_ref, rg_ref, gm_ref, gg_ref, gr_ref, wm_ref, wg_ref, wr_ref, y_ref):
    y = (jax.nn.sigmoid(gm_ref[...].astype(F32)) * _dot(om_ref[...], wm_ref[...])
         + jax.nn.sigmoid(gg_ref[...].astype(F32)) * _dot(og_ref[...], wg_ref[...])
         + jax.nn.sigmoid(gr_ref[...].astype(F32)) * _dot(rg_ref[...], wr_ref[...]))
    y_ref[...] = y.astype(y_ref.dtype)


def _merge(o_mla, o_gqa, rg, z, lp):
    r, d = o_mla.shape
    tm, tn = ROW_TILE, 512
    gate0 = C_GATE // tn
    per = d // tn
    act = pl.BlockSpec((tm, d), lambda i, j: (i, 0))
    wgt = pl.BlockSpec((d, tn), lambda i, j: (0, j))
    return pl.pallas_call(
        _merge_kernel,
        grid=(r // tm, d // tn),
        in_specs=[
            act, act, act,
            pl.BlockSpec((tm, tn), lambda i, j: (i, gate0 + j)),
            pl.BlockSpec((tm, tn), lambda i, j: (i, gate0 + per + j)),
            pl.BlockSpec((tm, tn), lambda i, j: (i, gate0 + 2 * per + j)),
            wgt, wgt, wgt,
        ],
        out_specs=pl.BlockSpec((tm, tn), lambda i, j: (i, j)),
        out_shape=jax.ShapeDtypeStruct((r, d), BF16),
        compiler_params=_params(("arbitrary", "arbitrary")),
        name="merge_branches",
    )(o_mla, o_gqa, rg, z, z, z, lp["mla_out"], lp["gqa_out"], lp["rg_out"])


def _resid_kernel(y_ref, w_ref, h_ref, ml_ref, mc_ref, o_ref, *, tm, tpb, s_lat, tn):
    i, j = pl.program_id(0), pl.program_id(1)
    is_ctx = _is_ctx_rows(i, tpb, tm, s_lat)
    col = pl.ds(pl.multiple_of(j * tn, tn), tn)
    gate = jnp.where(is_ctx, mc_ref[0, 2:3, col], ml_ref[0, 2:3, col])
    o_ref[...] = h_ref[...] + gate * _dot(y_ref[...], w_ref[...])


def _merge_out(y, w, h, mods, *, s_lat, s_tot):
    r, d = h.shape
    tm, tn = ROW_TILE, 512
    tpb = s_tot // tm
    return pl.pallas_call(
        functools.partial(_resid_kernel, tm=tm, tpb=tpb, s_lat=s_lat, tn=tn),
        grid=(r // tm, d // tn),
        in_specs=[
            pl.BlockSpec((tm, d), lambda i, j: (i, 0)),
            pl.BlockSpec((d, tn), lambda i, j: (0, j)),
            pl.BlockSpec((tm, tn), lambda i, j: (i, j)),
            pl.BlockSpec((1, 6, d), lambda i, j: (i // tpb, 0, 0)),
            pl.BlockSpec((1, 6, d), lambda i, j: (2, 0, 0)),
        ],
        out_specs=pl.BlockSpec((tm, tn), lambda i, j: (i, j)),
        out_shape=jax.ShapeDtypeStruct((r, d), F32),
        compiler_params=_params(("arbitrary", "arbitrary")),
        name="merge_out_residual",
    )(y, w, h, mods, mods)


def _ffn_kernel(h_ref, nw_ref, ml_ref, mc_ref, wg_ref, wu_ref, wd_ref, fw_ref, o_ref, u_scr,
                *, tm, tpb, s_lat, final_norm):
    i, j = pl.program_id(0), pl.program_id(1)
    o_ref = o_ref.at[0]

    @pl.when(j == 0)
    def _():
        _mod_norm_to(u_scr, h_ref, nw_ref, ml_ref, mc_ref, i, tpb, tm, s_lat, 3, 4)

    u = u_scr[...]
    g = _dot(u, wg_ref[...])
    g = g * jax.nn.sigmoid(g) * _dot(u, wu_ref[...])
    part = _dot(g.astype(BF16), wd_ref[...])

    @pl.when(j == 0)
    def _():
        o_ref[...] = part

    @pl.when(j > 0)
    def _():
        o_ref[...] += part

    @pl.when(j == pl.num_programs(1) - 1)
    def _():
        fw = fw_ref[...]
        for r0 in range(0, tm, ROW_CHUNK):
            rows = slice(r0, r0 + ROW_CHUNK)
            is_ctx = _is_ctx_rows(i, tpb, tm, s_lat, r0, ROW_CHUNK)
            gate = jnp.where(is_ctx, mc_ref[0, 5:6], ml_ref[0, 5:6])
            out = h_ref[rows, :] + gate * o_ref[rows, :]
            if final_norm:
                out = _rms(out, fw)
            o_ref[rows, :] = out


def _ffn(h, nw, mods, wg, wu, wd, fw, *, s_lat, s_tot, final_norm):
    r, d = h.shape
    out_rows = s_lat if final_norm else s_tot
    hid = wg.shape[1]
    tm, th = ROW_TILE, 512
    tpb = s_tot // tm
    kern = functools.partial(_ffn_kernel, tm=tm, tpb=tpb, s_lat=s_lat, final_norm=final_norm)
    return pl.pallas_call(
        kern,
        grid=(r // tm, hid // th),
        in_specs=[
            pl.BlockSpec((tm, d), lambda i, j: (i, 0)),
            pl.BlockSpec((1, d), lambda i, j: (0, 0)),
            pl.BlockSpec((1, 6, d), lambda i, j: (i // tpb, 0, 0)),
            pl.BlockSpec((1, 6, d), lambda i, j: (2, 0, 0)),
            pl.BlockSpec((d, th), lambda i, j: (0, j)),
            pl.BlockSpec((d, th), lambda i, j: (0, j)),
            pl.BlockSpec((th, d), lambda i, j: (j, 0)),
            pl.BlockSpec((1, d), lambda i, j: (0, 0)),
        ],
        out_specs=pl.BlockSpec((1, tm, d), lambda i, j: (i // tpb, i % tpb, 0)),
        out_shape=jax.ShapeDtypeStruct((r // s_tot, out_rows, d), F32),
        scratch_shapes=[pltpu.VMEM((tm, d), BF16)],
        compiler_params=_params(("arbitrary", "arbitrary")),
        name="ffn",
    )(h, nw, mods, mods, wg, wu, wd, fw)


def _rope_tables(s_lat, n_ctx, dim):
    quarter = dim // 4
    inv = ROPE_THETA ** (-jnp.arange(quarter, dtype=F32) / quarter)
    t = jnp.arange(s_lat)
    ang_r = (t // GRID_W).astype(F32)[:, None] * inv
    ang_c = (t % GRID_W).astype(F32)[:, None] * inv
    cos = jnp.concatenate([jnp.cos(ang_r)] * 2 + [jnp.cos(ang_c)] * 2, axis=1)
    sin = jnp.concatenate([-jnp.sin(ang_r), jnp.sin(ang_r), -jnp.sin(ang_c), jnp.sin(ang_c)], axis=1)
    cos = jnp.concatenate([cos, jnp.ones((n_ctx, dim), F32)], axis=0)
    sin = jnp.concatenate([sin, jnp.zeros((n_ctx, dim), F32)], axis=0)
    padw = LANE - dim
    if padw:
        cos = jnp.pad(cos, ((0, 0), (0, padw)))
        sin = jnp.pad(sin, ((0, 0), (0, padw)))
    return cos, sin


def _pad_w_in(w):
    d = w.shape[0]
    z = lambda n: jnp.zeros((d, n), w.dtype)
    ckv_kr = w[:, 0:576]
    gk_gv = w[:, 576:1600]
    rx = w[:, 1600:3648]
    cq = w[:, 3648:4160]
    gq = w[:, 4160:6208]
    ry = w[:, 6208:8256]
    gate = w[:, 8256:14400]
    return jnp.concatenate([ckv_kr, z(64), gk_gv, z(384), rx, gq, ry, gate, cq], axis=1).astype(BF16)


def _pad_q_up(w):
    rank = w.shape[0]
    w = w.reshape(rank, MLA_HEADS, MLA_NOPE + MLA_ROPE)
    w = jnp.pad(w, ((0, 0), (0, 0), (0, MLA_QK_PAD - MLA_NOPE - MLA_ROPE)))
    return w.reshape(rank, MLA_HEADS * MLA_QK_PAD).astype(BF16)


def kernel(x, c, ctx, c_ctx, mod_w, mod_b, norm_mix_w, norm_ffn_w, w_in, mla_q_norm_w, mla_q_up,
           mla_kv_norm_w, mla_kv_up, mla_out, gqa_q_norm_w, gqa_k_norm_w, gqa_out, rg_conv_w, rg_conv_b,
           rg_a_w, rg_a_b, rg_x_w, rg_x_b, rg_lambda, rg_out, merge_out, ffn_w_gate, ffn_w_up, ffn_w_down,
           final_norm_w):
    batch, s_lat, d = x.shape
    n_ctx = ctx.shape[1]
    depth = mod_w.shape[0]
    s_tot = s_lat + n_ctx
    assert batch == 2 and d == D_MODEL and n_ctx == SCAN_TILE
    assert s_tot % ROW_TILE == 0 and s_tot % PREP_TILE == 0 and s_lat % SCAN_TILE == 0 and s_lat % GRID_W == 0

    h = jnp.concatenate([x, ctx], axis=1).reshape(batch * s_tot, d)
    cs = jnp.zeros((8, d), F32).at[0:batch].set(c).at[batch].set(c_ctx)
    mods_all = _mods(cs, mod_w, mod_b).reshape(depth, 8, 6, d)
    tabs = _rope_tables(s_lat, n_ctx, MLA_ROPE) + _rope_tables(s_lat, n_ctx, GQA_HD)
    row2 = lambda v: v.reshape(1, -1)
    dims = dict(s_lat=s_lat, s_tot=s_tot)

    for l in range(depth):
        last = l == depth - 1
        mods = mods_all[l]
        lp = {
            "mla_kv_norm_w": row2(mla_kv_norm_w[l]), "mla_q_norm_w": row2(mla_q_norm_w[l]),
            "mla_kv_up": mla_kv_up[l].astype(BF16), "mla_q_up": _pad_q_up(mla_q_up[l]),
            "gqa_k_norm_w": row2(gqa_k_norm_w[l]), "gqa_q_norm_w": row2(gqa_q_norm_w[l]),
            "rg_conv_w": rg_conv_w[l], "rg_conv_b": row2(rg_conv_b[l]),
            "rg_a_w": rg_a_w[l].astype(BF16), "rg_a_b": rg_a_b[l],
            "rg_x_w": rg_x_w[l].astype(BF16), "rg_x_b": rg_x_b[l], "rg_lambda": rg_lambda[l],
            "mla_out": mla_out[l].astype(BF16), "gqa_out": gqa_out[l].astype(BF16),
            "rg_out": rg_out[l].astype(BF16),
        }
        z = _inproj(h, row2(norm_mix_w[l]), mods, _pad_w_in(w_in[l]), **dims)
        kmla, vtm, qtm, kg, vtg, qtg = _prep(z, lp, tabs, s_tot=s_tot, batch=batch)
        o_mla = _attention(qtm, kmla, vtm, heads=1, tq=MLA_Q_TILE, name="mla_attention", **dims)
        o_gqa = _attention(qtg, kg, vtg, heads=GQA_GROUP, tq=GQA_Q_TILE, name="gqa_attention", **dims)
        o_mla = o_mla.reshape(batch * s_tot, d)
        o_gqa = o_gqa.reshape(batch * s_tot, d)
        rg = _rglru(z, lp, batch=batch, **dims)
        y = _merge(o_mla, o_gqa, rg, z, lp)
        h = _merge_out(y, merge_out[l].astype(BF16), h, mods, **dims)
        h = _ffn(h, row2(norm_ffn_w[l]), mods, ffn_w_gate[l].astype(BF16), ffn_w_up[l].astype(BF16),
                 ffn_w_down[l].astype(BF16), row2(final_norm_w), final_norm=last, **dims)
        if not last:
            h = h.reshape(batch * s_tot, d)

    return h
```

```python
import functools
import math

import jax
import jax.numpy as jnp
from jax import lax
from jax.experimental import pallas as pl
from jax.experimental.pallas import tpu as pltpu

F32 = jnp.float32
BF16 = jnp.bfloat16

D_MODEL = 2048
GRID_W = 64
ROPE_THETA = 10000.0
NORM_EPS = 1e-6

MLA_HEADS = 16
MLA_RANK = 512
MLA_NOPE = 128
MLA_ROPE = 64
MLA_V = 128
MLA_QK_PAD = 256

GQA_HEADS = 16
GQA_KV = 4
GQA_GROUP = GQA_HEADS // GQA_KV
GQA_HD = 128

RNN_BLOCKS = 8
RNN_BW = D_MODEL // RNN_BLOCKS
CONV_W = 4
LRU_C = 8.0
FFN_HIDDEN = 5632

C_CKV = 0
C_KR = 512
C_GK = 640
C_GV = 1152
C_RX = 2048
C_GQ = 4096
C_RY = 6144
C_GATE = 8192
C_CQ = 14336
Z_COLS = 14848

LANE = 128
SUBLANES = 8
VMEM_CAP = 64 * 1024 * 1024
VMEM_LIMIT = 56 * 1024 * 1024

ROW_TILE = 768
ROW_CHUNK = 128
PREP_TILE = 384
Q_STRIP = 256
GQA_Q_TILE = 1024
MLA_Q_TILE = 4096
KV_CHUNKS = (1408, 768)
INPROJ_TILES = (1408, 768)
VT_ROWS = 144
SCAN_TILE = 256
LOG2E = 1.4426950408889634


def _params(sem):
    return pltpu.CompilerParams(dimension_semantics=sem, vmem_limit_bytes=VMEM_LIMIT)


def _dot(a, b):
    return jnp.dot(a, b, preferred_element_type=F32)


def _first_divisor(candidates, n):
    return next(c for c in candidates if n % c == 0)


def _rms(x, w):
    return x * lax.rsqrt(jnp.mean(x * x, axis=-1, keepdims=True) + NORM_EPS) * w


def _is_ctx_rows(tile_idx, tiles_per_batch, tm, s_lat, r0=0, n=None):
    n = tm if n is None else n
    r = (tile_idx % tiles_per_batch) * tm + r0 + lax.broadcasted_iota(jnp.int32, (n, 1), 0)
    return r >= s_lat


def _mod_norm_to(u_scr, h_ref, nw_ref, ml_ref, mc_ref, tile_idx, tpb, tm, s_lat, k_shift, k_scale):
    nw = nw_ref[...]
    for r0 in range(0, tm, ROW_CHUNK):
        is_ctx = _is_ctx_rows(tile_idx, tpb, tm, s_lat, r0, ROW_CHUNK)
        y = _rms(h_ref[r0:r0 + ROW_CHUNK, :], nw)
        shift = jnp.where(is_ctx, mc_ref[0, k_shift:k_shift + 1], ml_ref[0, k_shift:k_shift + 1])
        scale = jnp.where(is_ctx, mc_ref[0, k_scale:k_scale + 1], ml_ref[0, k_scale:k_scale + 1])
        u_scr[r0:r0 + ROW_CHUNK, :] = (y * (1.0 + scale) + shift).astype(BF16)


def _rope(x, cos, sin_signed, half):
    n = x.shape[-1]
    lane = lax.broadcasted_iota(jnp.int32, x.shape, 1)
    up = pltpu.roll(x, n - half, axis=1)
    dn = pltpu.roll(x, half, axis=1)
    partner = jnp.where((lane % (2 * half)) < half, up, dn)
    return x * cos + partner * sin_signed


def _mods_kernel(c_ref, w_ref, b_ref, o_ref):
    x = c_ref[...]
    x = x * jax.nn.sigmoid(x)
    w = w_ref[0]
    xh = x.astype(BF16)
    xl = (x - xh.astype(F32)).astype(BF16)
    wh = w.astype(BF16)
    wl = (w - wh.astype(F32)).astype(BF16)
    o_ref[0] = _dot(xh, wh) + _dot(xl, wh) + _dot(xh, wl) + b_ref[0]


def _mods(cs, mod_w, mod_b):
    depth, d, n = mod_w.shape
    tn = 512
    return pl.pallas_call(
        _mods_kernel,
        grid=(depth, n // tn),
        in_specs=[
            pl.BlockSpec((8, d), lambda l, j: (0, 0)),
            pl.BlockSpec((1, d, tn), lambda l, j: (l, 0, j)),
            pl.BlockSpec((1, 1, tn), lambda l, j: (l, 0, j)),
        ],
        out_specs=pl.BlockSpec((1, 8, tn), lambda l, j: (l, 0, j)),
        out_shape=jax.ShapeDtypeStruct((depth, 8, n), F32),
        compiler_params=_params(("arbitrary", "arbitrary")),
        name="adaln_mods",
    )(cs, mod_w, mod_b.reshape(depth, 1, n))


def _inproj_kernel(h_ref, nw_ref, ml_ref, mc_ref, w_ref, z_ref, u_scr, *, tm, tpb, s_lat):
    i = pl.program_id(0)

    @pl.when(pl.program_id(1) == 0)
    def _():
        _mod_norm_to(u_scr, h_ref, nw_ref, ml_ref, mc_ref, i, tpb, tm, s_lat, 0, 1)

    z_ref[...] = _dot(u_scr[...], w_ref[...]).astype(z_ref.dtype)


def _inproj(h, nw, mods, w_pad, *, s_lat, s_tot):
    r, d = h.shape
    tm, tn = _first_divisor(INPROJ_TILES, s_tot), 512
    tpb = s_tot // tm
    return pl.pallas_call(
        functools.partial(_inproj_kernel, tm=tm, tpb=tpb, s_lat=s_lat),
        grid=(r // tm, Z_COLS // tn),
        in_specs=[
            pl.BlockSpec((tm, d), lambda i, j: (i, 0)),
            pl.BlockSpec((1, d), lambda i, j: (0, 0)),
            pl.BlockSpec((1, 6, d), lambda i, j: (i // tpb, 0, 0)),
            pl.BlockSpec((1, 6, d), lambda i, j: (2, 0, 0)),
            pl.BlockSpec((d, tn), lambda i, j: (0, j)),
        ],
        out_specs=pl.BlockSpec((tm, tn), lambda i, j: (i, j)),
        out_shape=jax.ShapeDtypeStruct((r, Z_COLS), BF16),
        scratch_shapes=[pltpu.VMEM((tm, d), BF16)],
        compiler_params=_params(("arbitrary", "arbitrary")),
        name="in_proj",
    )(h, nw, mods, mods, w_pad)


def _prep_kernel(za_ref, cq_ref, gq_ref, kvw_ref, qw_ref, kvupk_ref, kvupvt_ref, qupt_ref, gkw_ref, gqw_ref,
                 cm_ref, sm_ref, cmt_ref, smt_ref, cg_ref, sg_ref,
                 kmla_ref, vtm_ref, qtm_ref, kg_ref, vtg_ref, qtg_ref):
    cos_m, sin_m = cm_ref[...], sm_ref[...]
    cos_g, sin_g = cg_ref[...], sg_ref[...]
    mla_scale = (MLA_NOPE + MLA_ROPE) ** -0.5 * LOG2E
    gqa_scale = GQA_HD ** -0.5 * LOG2E
    tm = za_ref.shape[0]

    ckv_f = _rms(za_ref[:, C_CKV:C_CKV + MLA_RANK].astype(F32), kvw_ref[...])
    ckv = ckv_f.astype(BF16)
    ckv_t = ckv_f.T.astype(BF16)
    kr = _rope(za_ref[:, C_KR:C_KR + LANE].astype(F32), cos_m, sin_m, MLA_ROPE // 4).astype(BF16)
    ones = jnp.ones((VT_ROWS - GQA_HD, tm), BF16)
    for h in range(0, MLA_HEADS, 2):
        k2 = _dot(ckv, kvupk_ref[:, h * MLA_NOPE:(h + 2) * MLA_NOPE])
        v2 = _dot(kvupvt_ref[h * MLA_V:(h + 2) * MLA_V, :], ckv_t)
        for j in range(2):
            kmla_ref[0, h + j, :, 0:MLA_NOPE] = k2[:, j * MLA_NOPE:(j + 1) * MLA_NOPE].astype(BF16)
            kmla_ref[0, h + j, :, MLA_NOPE:MLA_QK_PAD] = kr
            vtm_ref[0, h + j, 0:MLA_V, :] = v2[j * MLA_V:(j + 1) * MLA_V, :].astype(BF16)
            vtm_ref[0, h + j, MLA_V:VT_ROWS, :] = ones

    cq_t = _rms(cq_ref[...].astype(F32), qw_ref[...]).T.astype(BF16)
    cos_t, sin_t = cmt_ref[...], smt_ref[...]
    q16 = MLA_ROPE // 4
    zero_rows = jnp.zeros((MLA_QK_PAD - MLA_NOPE - MLA_ROPE, tm), BF16)
    for h in range(MLA_HEADS):
        q = _dot(qupt_ref[h * MLA_QK_PAD:(h + 1) * MLA_QK_PAD, :], cq_t)
        x = q[MLA_NOPE:MLA_NOPE + MLA_ROPE]
        partner = jnp.concatenate([x[q16:2 * q16], x[0:q16], x[3 * q16:4 * q16], x[2 * q16:3 * q16]], axis=0)
        q_rope = x * cos_t + partner * sin_t
        qtm_ref[0, h, 0:MLA_NOPE, :] = (q[0:MLA_NOPE] * mla_scale).astype(BF16)
        qtm_ref[0, h, MLA_NOPE:MLA_NOPE + MLA_ROPE, :] = (q_rope * mla_scale).astype(BF16)
        qtm_ref[0, h, MLA_NOPE + MLA_ROPE:MLA_QK_PAD, :] = zero_rows

    for n in range(GQA_KV):
        k = _rms(za_ref[:, C_GK + n * GQA_HD:C_GK + (n + 1) * GQA_HD].astype(F32), gkw_ref[...])
        kg_ref[0, n] = _rope(k, cos_g, sin_g, GQA_HD // 4).astype(BF16)
        v = za_ref[:, C_GV + n * GQA_HD:C_GV + (n + 1) * GQA_HD]
        vtg_ref[0, n, 0:GQA_HD, :] = v.astype(F32).T.astype(BF16)
        vtg_ref[0, n, GQA_HD:VT_ROWS, :] = ones
    for h in range(GQA_HEADS):
        q = _rms(gq_ref[:, h * GQA_HD:(h + 1) * GQA_HD].astype(F32), gqw_ref[...])
        q = _rope(q, cos_g, sin_g, GQA_HD // 4) * gqa_scale
        qtg_ref[0, h] = q.T.astype(BF16)


def _prep(z, lp, tabs, *, s_tot, batch):
    tm = PREP_TILE
    tpb = s_tot // tm
    cos_m, sin_m, cos_g, sin_g = tabs
    cos_mt, sin_mt = cos_m[:, :MLA_ROPE].T, sin_m[:, :MLA_ROPE].T
    tab = lambda b, i: (i, 0)
    tab_t = lambda b, i: (0, i)
    const = lambda b, i: (0, 0)
    rows_out = lambda b, i: (b, 0, i, 0)
    cols_out = lambda b, i: (b, 0, 0, i)
    outs = [
        jax.ShapeDtypeStruct((batch, MLA_HEADS, s_tot, MLA_QK_PAD), BF16),
        jax.ShapeDtypeStruct((batch, MLA_HEADS, VT_ROWS, s_tot), BF16),
        jax.ShapeDtypeStruct((batch, MLA_HEADS, MLA_QK_PAD, s_tot), BF16),
        jax.ShapeDtypeStruct((batch, GQA_KV, s_tot, GQA_HD), BF16),
        jax.ShapeDtypeStruct((batch, GQA_KV, VT_ROWS, s_tot), BF16),
        jax.ShapeDtypeStruct((batch, GQA_HEADS, GQA_HD, s_tot), BF16),
    ]
    return pl.pallas_call(
        _prep_kernel,
        grid=(batch, tpb),
        in_specs=[
            pl.BlockSpec((tm, 2048), lambda b, i: (b * tpb + i, 0)),
            pl.BlockSpec((tm, MLA_RANK), lambda b, i: (b * tpb + i, C_CQ // MLA_RANK)),
            pl.BlockSpec((tm, 2048), lambda b, i: (b * tpb + i, C_GQ // 2048)),
            pl.BlockSpec((1, MLA_RANK), const),
            pl.BlockSpec((1, MLA_RANK), const),
            pl.BlockSpec((MLA_RANK, MLA_HEADS * MLA_NOPE), const),
            pl.BlockSpec((MLA_HEADS * MLA_V, MLA_RANK), const),
            pl.BlockSpec((MLA_HEADS * MLA_QK_PAD, MLA_RANK), const),
            pl.BlockSpec((1, GQA_HD), const),
            pl.BlockSpec((1, GQA_HD), const),
            pl.BlockSpec((tm, LANE), tab),
            pl.BlockSpec((tm, LANE), tab),
            pl.BlockSpec((MLA_ROPE, tm), tab_t),
            pl.BlockSpec((MLA_ROPE, tm), tab_t),
            pl.BlockSpec((tm, LANE), tab),
            pl.BlockSpec((tm, LANE), tab),
        ],
        out_specs=[
            pl.BlockSpec((1, MLA_HEADS, tm, MLA_QK_PAD), rows_out),
            pl.BlockSpec((1, MLA_HEADS, VT_ROWS, tm), cols_out),
            pl.BlockSpec((1, MLA_HEADS, MLA_QK_PAD, tm), cols_out),
            pl.BlockSpec((1, GQA_KV, tm, GQA_HD), rows_out),
            pl.BlockSpec((1, GQA_KV, VT_ROWS, tm), cols_out),
            pl.BlockSpec((1, GQA_HEADS, GQA_HD, tm), cols_out),
        ],
        out_shape=outs,
        compiler_params=_params(("arbitrary", "arbitrary")),
        name="attn_prep",
    )(z, z, z, lp["mla_kv_norm_w"], lp["mla_q_norm_w"], lp["mla_k_up"], lp["mla_v_up_t"], lp["mla_q_up_t"],
      lp["gqa_k_norm_w"], lp["gqa_q_norm_w"], cos_m, sin_m, cos_mt, sin_mt, cos_g, sin_g)


def _softmax_strip(s, m_old):
    m_new = jnp.maximum(m_old, jnp.max(s, axis=0, keepdims=True).astype(F32))
    alpha = jnp.exp2(m_old - m_new)
    p = jnp.exp2(s - m_new.astype(BF16))
    return m_new, alpha, p


def _scores(k, q):
    return _dot(k, q).astype(BF16)


def _attn_kernel(qt_ref, k_ref, vt_ref, o_ref, acc_scr, m_scr, s_scr, *, heads, tq, ch, n_chunks):
    n_strips = heads * tq // Q_STRIP

    def q_strip(g):
        h, off = divmod(g * Q_STRIP, tq)
        return qt_ref[0, h, :, off:off + Q_STRIP]

    def keys(c):
        return k_ref[0, 0, pl.ds(pl.multiple_of(c * ch, ch), ch), :]

    m_scr[...] = jnp.full(m_scr.shape, -1e30, F32)
    acc_scr[...] = jnp.zeros(acc_scr.shape, F32)

    def chunk(c, issue_next):
        k = keys(c)
        vt = vt_ref[0, 0, :, pl.ds(pl.multiple_of(c * ch, ch), ch)]
        m_all = m_scr[...]
        scores = [s_scr[...]] + [None] * (n_strips - 1)
        probs = [None] * n_strips
        for t in range(n_strips + 1):
            if t + 1 < n_strips:
                scores[t + 1] = _scores(k, q_strip(t + 1))
            elif t + 1 == n_strips and issue_next:
                s_scr[...] = _scores(keys(c + 1), q_strip(0))
            if t < n_strips:
                cols = slice(t * Q_STRIP, (t + 1) * Q_STRIP)
                m_new, alpha, p = _softmax_strip(scores[t], m_all[:, cols])
                probs[t] = (m_new, alpha, p)
            if t >= 1:
                cols = slice((t - 1) * Q_STRIP, t * Q_STRIP)
                m_new, alpha, p = probs[t - 1]
                m_scr[:, cols] = m_new
                acc_scr[:, cols] = alpha * acc_scr[:, cols] + _dot(vt, p)

    s_scr[...] = _scores(keys(0), q_strip(0))

    def body(c, carry):
        chunk(c, True)
        return carry

    lax.fori_loop(0, n_chunks - 1, body, 0)
    chunk(n_chunks - 1, False)

    inv_l = 1.0 / acc_scr[GQA_HD:GQA_HD + 1, :]
    for g in range(heads):
        for j in range(tq // Q_STRIP):
            cols = slice(g * tq + j * Q_STRIP, g * tq + (j + 1) * Q_STRIP)
            o = acc_scr[0:GQA_HD, cols] * inv_l[:, cols]
            o_ref[0, j * Q_STRIP:(j + 1) * Q_STRIP, g * GQA_HD:(g + 1) * GQA_HD] = o.T.astype(o_ref.dtype)


def _ctx_attn_kernel(qt_ref, k_ref, vt_ref, o_in_ref, o_ref, *, heads):
    del o_in_ref
    q = jnp.concatenate([qt_ref[0, g] for g in range(heads)], axis=1) if heads > 1 else qt_ref[0, 0]
    s = _scores(k_ref[0, 0], q)
    _, _, p = _softmax_strip(s, jnp.full((1, s.shape[1]), -1e30, F32))
    acc = _dot(vt_ref[0, 0], p)
    o = acc[0:GQA_HD] / acc[GQA_HD:GQA_HD + 1]
    n = o.shape[1] // heads
    for g in range(heads):
        o_ref[0, :, g * GQA_HD:(g + 1) * GQA_HD] = o[:, g * n:(g + 1) * n].T.astype(o_ref.dtype)


def _attention(qt, k, vt, *, heads, tq, s_lat, s_tot, name):
    batch, n_heads, dk, _ = qt.shape
    n_kv = k.shape[1]
    n_ctx = s_tot - s_lat
    tq = min(tq, s_lat)
    ch = _first_divisor(KV_CHUNKS, s_tot)
    assert s_lat % tq == 0 and tq % Q_STRIP == 0 and s_lat % n_ctx == 0
    kern = functools.partial(_attn_kernel, heads=heads, tq=tq, ch=ch, n_chunks=s_tot // ch)
    out_shape = jax.ShapeDtypeStruct((batch, s_tot, n_heads * GQA_HD), BF16)
    o = pl.pallas_call(
        kern,
        grid=(batch, n_kv, s_lat // tq),
        in_specs=[
            pl.BlockSpec((1, heads, dk, tq), lambda b, n, q: (b, n, 0, q)),
            pl.BlockSpec((1, 1, s_tot, dk), lambda b, n, q: (b, n, 0, 0)),
            pl.BlockSpec((1, 1, VT_ROWS, s_tot), lambda b, n, q: (b, n, 0, 0)),
        ],
        out_specs=pl.BlockSpec((1, tq, heads * GQA_HD), lambda b, n, q: (b, q, n)),
        out_shape=out_shape,
        scratch_shapes=[
            pltpu.VMEM((VT_ROWS, heads * tq), F32),
            pltpu.VMEM((1, heads * tq), F32),
            pltpu.VMEM((ch, Q_STRIP), BF16),
        ],
        compiler_params=_params(("arbitrary", "arbitrary", "arbitrary")),
        name=name,
    )(qt, k, vt)
    ctx_blk = s_lat // n_ctx
    return pl.pallas_call(
        functools.partial(_ctx_attn_kernel, heads=heads),
        grid=(batch, n_kv),
        in_specs=[
            pl.BlockSpec((1, heads, dk, n_ctx), lambda b, n: (b, n, 0, ctx_blk)),
            pl.BlockSpec((1, 1, n_ctx, dk), lambda b, n: (b, n, ctx_blk, 0)),
            pl.BlockSpec((1, 1, VT_ROWS, n_ctx), lambda b, n: (b, n, 0, ctx_blk)),
            pl.BlockSpec(memory_space=pl.ANY),
        ],
        out_specs=pl.BlockSpec((1, n_ctx, heads * GQA_HD), lambda b, n: (b, ctx_blk, n)),
        out_shape=out_shape,
        input_output_aliases={3: 0},
        compiler_params=_params(("arbitrary", "arbitrary")),
        name=name + "_ctx",
    )(qt, k, vt, o)


def _shift_rows(x, s, fill, reverse):
    n = x.shape[0]
    if s % SUBLANES == 0:
        pad = jnp.full((s, x.shape[1]), fill, x.dtype)
        return jnp.concatenate([x[s:], pad] if reverse else [pad, x[:n - s]], axis=0)
    t = lax.broadcasted_iota(jnp.int32, x.shape, 0)
    if reverse:
        return jnp.where(t < n - s, pltpu.roll(x, n - s, axis=0), fill)
    return jnp.where(t >= s, pltpu.roll(x, s, axis=0), fill)


def _tile_scan(a, b, h0, reverse):
    s = 1
    while s < a.shape[0]:
        b = a * _shift_rows(b, s, 0.0, reverse) + b
        a = a * _shift_rows(a, s, 1.0, reverse)
        s *= 2
    return a * h0 + b


def _gelu_tanh(x):
    return x * jax.nn.sigmoid(2.0 * math.sqrt(2.0 / math.pi) * (x + 0.044715 * x * x * x))


def _rglru_kernel(rx_ref, ry_ref, cw_ref, cb_ref, wa_ref, ba_ref, wx_ref, bx_ref, lam_ref,
                  o_ref, x_scr, hf_scr, *, s_lat, n_ctx):
    tt = SCAN_TILE
    x_scr[...] = rx_ref[...].astype(F32)

    cw = cw_ref[...]
    cb = cb_ref[...]
    lam = -lam_ref[...]
    softplus = jnp.log1p(jnp.exp(-jnp.abs(lam))) + jnp.maximum(lam, 0.0)
    zero_rows = jnp.zeros((SUBLANES, RNN_BW), F32)

    def conv_tile(row0, n, prev_tail, next_head):
        cur = x_scr[pl.ds(row0, n), :]
        ext = jnp.concatenate([prev_tail, cur, next_head], axis=0)
        xc = cb
        for j in range(CONV_W):
            o = SUBLANES - CONV_W // 2 + j
            xc = xc + ext[o:o + n, :] * cw[j:j + 1, :]
        x_scr[pl.ds(row0, n), :] = xc
        return xc, cur[n - SUBLANES:n, :]

    def coeffs(xc, d):
        xb = xc.astype(BF16)
        r = jax.nn.sigmoid(_dot(xb, wa_ref[d, 0]) + ba_ref[d:d + 1, :])
        i = jax.nn.sigmoid(_dot(xb, wx_ref[d, 0]) + bx_ref[d:d + 1, :])
        log_a = -LRU_C * r * softplus[d:d + 1, :]
        a = jnp.exp(log_a)
        y = jnp.maximum(jnp.tanh(-log_a) * (1.0 + a * a), 0.0)
        mult = jnp.where(y > 0.0, y * lax.rsqrt(y), 0.0)
        return a, mult * i * xc

    def fwd_tile(row0, n, xc, h0):
        a, b = coeffs(xc, 0)
        h = _tile_scan(a, b, h0, reverse=False)
        hf_scr[pl.ds(row0, n), :] = h
        return h[n - 1:n, :]

    def bwd_tile(row0, n, h0):
        a, b = coeffs(x_scr[pl.ds(row0, n), :], 1)
        h = _tile_scan(a, b, h0, reverse=True)
        y = ry_ref[pl.ds(row0, n), :].astype(F32)
        o_ref[pl.ds(row0, n), :] = ((hf_scr[pl.ds(row0, n), :] + h) * _gelu_tanh(y)).astype(o_ref.dtype)
        return h[0:1, :]

    n_lat_tiles = s_lat // tt
    h0 = jnp.zeros((1, RNN_BW), F32)

    xc, _ = conv_tile(s_lat, n_ctx, zero_rows, zero_rows)
    hc = fwd_tile(s_lat, n_ctx, xc, h0)

    def fwd(t, carry):
        h, tail = carry
        r0 = pl.multiple_of(t * tt, tt)
        nxt = jnp.where(t == n_lat_tiles - 1, 0.0, x_scr[pl.ds(r0 + tt, SUBLANES), :])
        xc, tail = conv_tile(r0, tt, tail, nxt)
        return fwd_tile(r0, tt, xc, h), tail

    lax.fori_loop(0, n_lat_tiles, fwd, (hc, zero_rows))

    hc = bwd_tile(s_lat, n_ctx, h0)

    def bwd(t, h):
        r0 = pl.multiple_of((n_lat_tiles - 1 - t) * tt, tt)
        return bwd_tile(r0, tt, h)

    lax.fori_loop(0, n_lat_tiles, bwd, hc)


def _rglru(z, lp, *, s_lat, s_tot, batch):
    r = z.shape[0]
    n_ctx = s_tot - s_lat
    bw = RNN_BW
    kern = functools.partial(_rglru_kernel, s_lat=s_lat, n_ctx=n_ctx)
    vec = lambda b, k: (0, k)
    return pl.pallas_call(
        kern,
        grid=(batch, RNN_BLOCKS),
        in_specs=[
            pl.BlockSpec((s_tot, bw), lambda b, k: (b, C_RX // bw + k)),
            pl.BlockSpec((s_tot, bw), lambda b, k: (b, C_RY // bw + k)),
            pl.BlockSpec((CONV_W, bw), vec),
            pl.BlockSpec((1, bw), vec),
            pl.BlockSpec((2, 1, bw, bw), lambda b, k: (0, k, 0, 0)),
            pl.BlockSpec((2, bw), vec),
            pl.BlockSpec((2, 1, bw, bw), lambda b, k: (0, k, 0, 0)),
            pl.BlockSpec((2, bw), vec),
            pl.BlockSpec((2, bw), vec),
        ],
        out_specs=pl.BlockSpec((s_tot, bw), lambda b, k: (b, k)),
        out_shape=jax.ShapeDtypeStruct((r, D_MODEL), BF16),
        scratch_shapes=[
            pltpu.VMEM((s_tot, bw), F32),
            pltpu.VMEM((s_tot, bw), F32),
        ],
        compiler_params=_params(("arbitrary", "arbitrary")),
        name="rglru",
    )(z, z, lp["rg_conv_w"], lp["rg_conv_b"], lp["rg_a_w"], lp["rg_a_b"], lp["rg_x_w"], lp["rg_x_b"],
      lp["rg_lambda"])


def _merge_kernel(om_ref, og_ref, rg_ref, gm_ref, gg_ref, gr_ref, wm_ref, wg_ref, wr_ref, y_ref):
    y = (jax.nn.sigmoid(gm_ref[...].astype(F32)) * _dot(om_ref[...], wm_ref[...])
         + jax.nn.sigmoid(gg_ref[...].astype(F32)) * _dot(og_ref[...], wg_ref[...])
         + jax.nn.sigmoid(gr_ref[...].astype(F32)) * _dot(rg_ref[...], wr_ref[...]))
    y_ref[...] = y.astype(y_ref.dtype)


def _merge(o_mla, o_gqa, rg, z, lp):
    r, d = o_mla.shape
    tm, tn = ROW_TILE, 512
    gate0 = C_GATE // tn
    per = d // tn
    act = pl.BlockSpec((tm, d), lambda i, j: (i, 0))
    wgt = pl.BlockSpec((d, tn), lambda i, j: (0, j))
    return pl.pallas_call(
        _merge_kernel,
        grid=(r // tm, d // tn),
        in_specs=[
            act, act, act,
            pl.BlockSpec((tm, tn), lambda i, j: (i, gate0 + j)),
            pl.BlockSpec((tm, tn), lambda i, j: (i, gate0 + per + j)),
            pl.BlockSpec((tm, tn), lambda i, j: (i, gate0 + 2 * per + j)),
            wgt, wgt, wgt,
        ],
        out_specs=pl.BlockSpec((tm, tn), lambda i, j: (i, j)),
        out_shape=jax.ShapeDtypeStruct((r, d), BF16),
        compiler_params=_params(("arbitrary", "arbitrary")),
        name="merge_branches",
    )(o_mla, o_gqa, rg, z, z, z, lp["mla_out"], lp["gqa_out"], lp["rg_out"])


def _resid_kernel(y_ref, w_ref, h_ref, ml_ref, mc_ref, o_ref, *, tm, tpb, s_lat, tn):
    i, j = pl.program_id(0), pl.program_id(1)
    is_ctx = _is_ctx_rows(i, tpb, tm, s_lat)
    col = pl.ds(pl.multiple_of(j * tn, tn), tn)
    gate = jnp.where(is_ctx, mc_ref[0, 2:3, col], ml_ref[0, 2:3, col])
    o_ref[...] = h_ref[...] + gate * _dot(y_ref[...], w_ref[...])


def _merge_out(y, w, h, mods, *, s_lat, s_tot):
    r, d = h.shape
    tm, tn = ROW_TILE, 1024
    tpb = s_tot // tm
    return pl.pallas_call(
        functools.partial(_resid_kernel, tm=tm, tpb=tpb, s_lat=s_lat, tn=tn),
        grid=(r // tm, d // tn),
        in_specs=[
            pl.BlockSpec((tm, d), lambda i, j: (i, 0)),
            pl.BlockSpec((d, tn), lambda i, j: (0, j)),
            pl.BlockSpec((tm, tn), lambda i, j: (i, j)),
            pl.BlockSpec((1, 6, d), lambda i, j: (i // tpb, 0, 0)),
            pl.BlockSpec((1, 6, d), lambda i, j: (2, 0, 0)),
        ],
        out_specs=pl.BlockSpec((tm, tn), lambda i, j: (i, j)),
        out_shape=jax.ShapeDtypeStruct((r, d), F32),
        compiler_params=_params(("arbitrary", "arbitrary")),
        name="merge_out_residual",
    )(y, w, h, mods, mods)


def _ffn_kernel(h_ref, nw_ref, ml_ref, mc_ref, wg_ref, wu_ref, wd_ref, fw_ref, o_ref, u_scr,
                *, tm, tpb, s_lat, final_norm):
    i, j = pl.program_id(0), pl.program_id(1)
    o_ref = o_ref.at[0]

    @pl.when(j == 0)
    def _():
        _mod_norm_to(u_scr, h_ref, nw_ref, ml_ref, mc_ref, i, tpb, tm, s_lat, 3, 4)

    u = u_scr[...]
    g = _dot(u, wg_ref[...])
    g = g * jax.nn.sigmoid(g) * _dot(u, wu_ref[...])
    part = _dot(g.astype(BF16), wd_ref[...])

    @pl.when(j == 0)
    def _():
        o_ref[...] = part

    @pl.when(j > 0)
    def _():
        o_ref[...] += part

    @pl.when(j == pl.num_programs(1) - 1)
    def _():
        fw = fw_ref[...]
        for r0 in range(0, tm, ROW_CHUNK):
            rows = slice(r0, r0 + ROW_CHUNK)
            is_ctx = _is_ctx_rows(i, tpb, tm, s_lat, r0, ROW_CHUNK)
            gate = jnp.where(is_ctx, mc_ref[0, 5:6], ml_ref[0, 5:6])
            out = h_ref[rows, :] + gate * o_ref[rows, :]
            if final_norm:
                out = _rms(out, fw)
            o_ref[rows, :] = out


def _ffn(h, nw, mods, wg, wu, wd, fw, *, s_lat, s_tot, final_norm):
    r, d = h.shape
    out_rows = s_lat if final_norm else s_tot
    hid = wg.shape[1]
    tm, th = ROW_TILE, 512
    tpb = s_tot // tm
    kern = functools.partial(_ffn_kernel, tm=tm, tpb=tpb, s_lat=s_lat, final_norm=final_norm)
    return pl.pallas_call(
        kern,
        grid=(r // tm, hid // th),
        in_specs=[
            pl.BlockSpec((tm, d), lambda i, j: (i, 0)),
            pl.BlockSpec((1, d), lambda i, j: (0, 0)),
            pl.BlockSpec((1, 6, d), lambda i, j: (i // tpb, 0, 0)),
            pl.BlockSpec((1, 6, d), lambda i, j: (2, 0, 0)),
            pl.BlockSpec((d, th), lambda i, j: (0, j)),
            pl.BlockSpec((d, th), lambda i, j: (0, j)),
            pl.BlockSpec((th, d), lambda i, j: (j, 0)),
            pl.BlockSpec((1, d), lambda i, j: (0, 0)),
        ],
        out_specs=pl.BlockSpec((1, tm, d), lambda i, j: (i // tpb, i % tpb, 0)),
        out_shape=jax.ShapeDtypeStruct((r // s_tot, out_rows, d), F32),
        scratch_shapes=[pltpu.VMEM((tm, d), BF16)],
        compiler_params=_params(("arbitrary", "arbitrary")),
        name="ffn",
    )(h, nw, mods, mods, wg, wu, wd, fw)


def _rope_tables(s_lat, n_ctx, dim):
    quarter = dim // 4
    inv = ROPE_THETA ** (-jnp.arange(quarter, dtype=F32) / quarter)
    rows = s_lat // GRID_W
    ang_r = jnp.arange(rows, dtype=F32)[:, None] * inv
    ang_c = jnp.arange(GRID_W, dtype=F32)[:, None] * inv
    by_row = lambda v: jnp.repeat(v, GRID_W, axis=0)
    by_col = lambda v: jnp.tile(v, (rows, 1))
    cos_r, sin_r, cos_c, sin_c = by_row(jnp.cos(ang_r)), by_row(jnp.sin(ang_r)), by_col(jnp.cos(ang_c)), by_col(jnp.sin(ang_c))
    cos = jnp.concatenate([cos_r, cos_r, cos_c, cos_c], axis=1)
    sin = jnp.concatenate([-sin_r, sin_r, -sin_c, sin_c], axis=1)
    cos = jnp.concatenate([cos, jnp.ones((n_ctx, dim), F32)], axis=0)
    sin = jnp.concatenate([sin, jnp.zeros((n_ctx, dim), F32)], axis=0)
    padw = LANE - dim
    if padw:
        cos = jnp.pad(cos, ((0, 0), (0, padw)))
        sin = jnp.pad(sin, ((0, 0), (0, padw)))
    return cos, sin


def _pad_w_in(w):
    d = w.shape[0]
    z = lambda n: jnp.zeros((d, n), w.dtype)
    ckv_kr = w[:, 0:576]
    gk_gv = w[:, 576:1600]
    rx = w[:, 1600:3648]
    cq = w[:, 3648:4160]
    gq = w[:, 4160:6208]
    ry = w[:, 6208:8256]
    gate = w[:, 8256:14400]
    return jnp.concatenate([ckv_kr, z(64), gk_gv, z(384), rx, gq, ry, gate, cq], axis=1).astype(BF16)


def _split_kv_up(w):
    rank = w.shape[0]
    w = w.reshape(rank, MLA_HEADS, MLA_NOPE + MLA_V).astype(BF16)
    k = w[:, :, :MLA_NOPE].reshape(rank, MLA_HEADS * MLA_NOPE)
    v = w[:, :, MLA_NOPE:].reshape(rank, MLA_HEADS * MLA_V)
    return k, v.T


def _pad_q_up(w):
    rank = w.shape[0]
    w = w.reshape(rank, MLA_HEADS, MLA_NOPE + MLA_ROPE)
    w = jnp.pad(w, ((0, 0), (0, 0), (0, MLA_QK_PAD - MLA_NOPE - MLA_ROPE)))
    return w.reshape(rank, MLA_HEADS * MLA_QK_PAD).astype(BF16)


def kernel(x, c, ctx, c_ctx, mod_w, mod_b, norm_mix_w, norm_ffn_w, w_in, mla_q_norm_w, mla_q_up,
           mla_kv_norm_w, mla_kv_up, mla_out, gqa_q_norm_w, gqa_k_norm_w, gqa_out, rg_conv_w, rg_conv_b,
           rg_a_w, rg_a_b, rg_x_w, rg_x_b, rg_lambda, rg_out, merge_out, ffn_w_gate, ffn_w_up, ffn_w_down,
           final_norm_w):
    batch, s_lat, d = x.shape
    n_ctx = ctx.shape[1]
    depth = mod_w.shape[0]
    s_tot = s_lat + n_ctx
    assert batch == 2 and d == D_MODEL and n_ctx == SCAN_TILE
    assert s_tot % ROW_TILE == 0 and s_tot % PREP_TILE == 0 and s_lat % SCAN_TILE == 0 and s_lat % GRID_W == 0

    h = jnp.concatenate([x, ctx], axis=1).reshape(batch * s_tot, d)
    cs = jnp.zeros((8, d), F32).at[0:batch].set(c).at[batch].set(c_ctx)
    mods_all = _mods(cs, mod_w, mod_b).reshape(depth, 8, 6, d)
    tabs = _rope_tables(s_lat, n_ctx, MLA_ROPE) + _rope_tables(s_lat, n_ctx, GQA_HD)
    row2 = lambda v: v.reshape(1, -1)
    dims = dict(s_lat=s_lat, s_tot=s_tot)

    for l in range(depth):
        last = l == depth - 1
        mods = mods_all[l]
        lp = {
            "mla_kv_norm_w": row2(mla_kv_norm_w[l]), "mla_q_norm_w": row2(mla_q_norm_w[l]),
            "mla_k_up": _split_kv_up(mla_kv_up[l])[0], "mla_v_up_t": _split_kv_up(mla_kv_up[l])[1],
            "mla_q_up_t": _pad_q_up(mla_q_up[l]).T,
            "gqa_k_norm_w": row2(gqa_k_norm_w[l]), "gqa_q_norm_w": row2(gqa_q_norm_w[l]),
            "rg_conv_w": rg_conv_w[l], "rg_conv_b": row2(rg_conv_b[l]),
            "rg_a_w": rg_a_w[l].astype(BF16), "rg_a_b": rg_a_b[l],
            "rg_x_w": rg_x_w[l].astype(BF16), "rg_x_b": rg_x_b[l], "rg_lambda": rg_lambda[l],
            "mla_out": mla_out[l].astype(BF16), "gqa_out": gqa_out[l].astype(BF16),
            "rg_out": rg_out[l].astype(BF16),
        }
        z = _inproj(h, row2(norm_mix_w[l]), mods, _pad_w_in(w_in[l]), **dims)
        kmla, vtm, qtm, kg, vtg, qtg = _prep(z, lp, tabs, s_tot=s_tot, batch=batch)
        o_mla = _attention(qtm, kmla, vtm, heads=1, tq=MLA_Q_TILE, name="mla_attention", **dims)
        o_gqa = _attention(qtg, kg, vtg, heads=GQA_GROUP, tq=GQA_Q_TILE, name="gqa_attention", **dims)
        o_mla = o_mla.reshape(batch * s_tot, d)
        o_gqa = o_gqa.reshape(batch * s_tot, d)
        rg = _rglru(z, lp, batch=batch, **dims)
        y = _merge(o_mla, o_gqa, rg, z, lp)
        h = _merge_out(y, merge_out[l].astype(BF16), h, mods, **dims)
        h = _ffn(h, row2(norm_ffn_w[l]), mods, ffn_w_gate[l].astype(BF16), ffn_w_up[l].astype(BF16),
                 ffn_w_down[l].astype(BF16), row2(final_norm_w), final_norm=last, **dims)
        if not last:
            h = h.reshape(batch * s_tot, d)

    return h
```

```python
import functools
import math

import jax
import jax.numpy as jnp
from jax import lax
from jax.experimental import pallas as pl
from jax.experimental.pallas import tpu as pltpu

F32 = jnp.float32
BF16 = jnp.bfloat16

D_MODEL = 2048
GRID_W = 64
ROPE_THETA = 10000.0
NORM_EPS = 1e-6

MLA_HEADS = 16
MLA_RANK = 512
MLA_NOPE = 128
MLA_ROPE = 64
MLA_V = 128
MLA_QK_PAD = 256

GQA_HEADS = 16
GQA_KV = 4
GQA_GROUP = GQA_HEADS // GQA_KV
GQA_HD = 128

RNN_BLOCKS = 8
RNN_BW = D_MODEL // RNN_BLOCKS
CONV_W = 4
LRU_C = 8.0
FFN_HIDDEN = 5632

C_CKV = 0
C_KR = 512
C_GK = 640
C_GV = 1152
C_RX = 2048
C_GQ = 4096
C_RY = 6144
C_GATE = 8192
C_CQ = 14336
Z_COLS = 14848

LANE = 128
SUBLANES = 8
VMEM_CAP = 64 * 1024 * 1024
VMEM_LIMIT = 56 * 1024 * 1024

ROW_TILE = 768
ROW_CHUNK = 128
PREP_TILE = 384
Q_STRIP = 256
GQA_Q_TILE = 2048
MLA_Q_TILE = 8192
KV_CHUNKS = (1408, 768)
INPROJ_TILES = (1408, 768)
VT_ROWS = 144
SCAN_TILE = 256
LOG2E = 1.4426950408889634


def _params(sem):
    return pltpu.CompilerParams(dimension_semantics=sem, vmem_limit_bytes=VMEM_LIMIT)


def _dot(a, b):
    return jnp.dot(a, b, preferred_element_type=F32)


def _first_divisor(candidates, n):
    return next(c for c in candidates if n % c == 0)


def _rms(x, w):
    return x * lax.rsqrt(jnp.mean(x * x, axis=-1, keepdims=True) + NORM_EPS) * w


def _is_ctx_rows(tile_idx, tiles_per_batch, tm, s_lat, r0=0, n=None):
    n = tm if n is None else n
    r = (tile_idx % tiles_per_batch) * tm + r0 + lax.broadcasted_iota(jnp.int32, (n, 1), 0)
    return r >= s_lat


def _mod_norm_to(u_scr, h_ref, nw_ref, ml_ref, mc_ref, tile_idx, tpb, tm, s_lat, k_shift, k_scale):
    nw = nw_ref[...]
    for r0 in range(0, tm, ROW_CHUNK):
        is_ctx = _is_ctx_rows(tile_idx, tpb, tm, s_lat, r0, ROW_CHUNK)
        y = _rms(h_ref[r0:r0 + ROW_CHUNK, :], nw)
        shift = jnp.where(is_ctx, mc_ref[0, k_shift:k_shift + 1], ml_ref[0, k_shift:k_shift + 1])
        scale = jnp.where(is_ctx, mc_ref[0, k_scale:k_scale + 1], ml_ref[0, k_scale:k_scale + 1])
        u_scr[r0:r0 + ROW_CHUNK, :] = (y * (1.0 + scale) + shift).astype(BF16)


def _rope(x, cos, sin_signed, half):
    n = x.shape[-1]
    lane = lax.broadcasted_iota(jnp.int32, x.shape, 1)
    up = pltpu.roll(x, n - half, axis=1)
    dn = pltpu.roll(x, half, axis=1)
    partner = jnp.where((lane % (2 * half)) < half, up, dn)
    return x * cos + partner * sin_signed


def _mods_kernel(c_ref, w_ref, b_ref, o_ref):
    x = c_ref[...]
    x = x * jax.nn.sigmoid(x)
    w = w_ref[0]
    xh = x.astype(BF16)
    xl = (x - xh.astype(F32)).astype(BF16)
    wh = w.astype(BF16)
    wl = (w - wh.astype(F32)).astype(BF16)
    o_ref[0] = _dot(xh, wh) + _dot(xl, wh) + _dot(xh, wl) + b_ref[0]


def _mods(cs, mod_w, mod_b):
    depth, d, n = mod_w.shape
    tn = 512
    return pl.pallas_call(
        _mods_kernel,
        grid=(depth, n // tn),
        in_specs=[
            pl.BlockSpec((8, d), lambda l, j: (0, 0)),
            pl.BlockSpec((1, d, tn), lambda l, j: (l, 0, j)),
            pl.BlockSpec((1, 1, tn), lambda l, j: (l, 0, j)),
        ],
        out_specs=pl.BlockSpec((1, 8, tn), lambda l, j: (l, 0, j)),
        out_shape=jax.ShapeDtypeStruct((depth, 8, n), F32),
        compiler_params=_params(("arbitrary", "arbitrary")),
        name="adaln_mods",
    )(cs, mod_w, mod_b.reshape(depth, 1, n))


def _inproj_kernel(h_ref, nw_ref, ml_ref, mc_ref, w_ref, z_ref, u_scr, *, tm, tpb, s_lat):
    i = pl.program_id(0)

    @pl.when(pl.program_id(1) == 0)
    def _():
        _mod_norm_to(u_scr, h_ref, nw_ref, ml_ref, mc_ref, i, tpb, tm, s_lat, 0, 1)

    z_ref[...] = _dot(u_scr[...], w_ref[...]).astype(z_ref.dtype)


def _inproj(h, nw, mods, w_pad, *, s_lat, s_tot):
    r, d = h.shape
    tm, tn = _first_divisor(INPROJ_TILES, s_tot), 512
    tpb = s_tot // tm
    return pl.pallas_call(
        functools.partial(_inproj_kernel, tm=tm, tpb=tpb, s_lat=s_lat),
        grid=(r // tm, Z_COLS // tn),
        in_specs=[
            pl.BlockSpec((tm, d), lambda i, j: (i, 0)),
            pl.BlockSpec((1, d), lambda i, j: (0, 0)),
            pl.BlockSpec((1, 6, d), lambda i, j: (i // tpb, 0, 0)),
            pl.BlockSpec((1, 6, d), lambda i, j: (2, 0, 0)),
            pl.BlockSpec((d, tn), lambda i, j: (0, j)),
        ],
        out_specs=pl.BlockSpec((tm, tn), lambda i, j: (i, j)),
        out_shape=jax.ShapeDtypeStruct((r, Z_COLS), BF16),
        scratch_shapes=[pltpu.VMEM((tm, d), BF16)],
        compiler_params=_params(("arbitrary", "arbitrary")),
        name="in_proj",
    )(h, nw, mods, mods, w_pad)


def _prep_kernel(za_ref, cq_ref, gq_ref, kvw_ref, qw_ref, kvupk_ref, kvupvt_ref, qupt_ref, gkw_ref, gqw_ref,
                 cm_ref, sm_ref, cmt_ref, smt_ref, cg_ref, sg_ref,
                 kmla_ref, vtm_ref, qtm_ref, kg_ref, vtg_ref, qtg_ref):
    cos_m, sin_m = cm_ref[...], sm_ref[...]
    cos_g, sin_g = cg_ref[...], sg_ref[...]
    mla_scale = (MLA_NOPE + MLA_ROPE) ** -0.5 * LOG2E
    gqa_scale = GQA_HD ** -0.5 * LOG2E
    tm = za_ref.shape[0]

    ckv_f = _rms(za_ref[:, C_CKV:C_CKV + MLA_RANK].astype(F32), kvw_ref[...])
    ckv = ckv_f.astype(BF16)
    ckv_t = ckv_f.T.astype(BF16)
    kr = _rope(za_ref[:, C_KR:C_KR + LANE].astype(F32), cos_m, sin_m, MLA_ROPE // 4).astype(BF16)
    ones = jnp.ones((VT_ROWS - GQA_HD, tm), BF16)
    for h in range(0, MLA_HEADS, 2):
        k2 = _dot(ckv, kvupk_ref[:, h * MLA_NOPE:(h + 2) * MLA_NOPE])
        v2 = _dot(kvupvt_ref[h * MLA_V:(h + 2) * MLA_V, :], ckv_t)
        for j in range(2):
            kmla_ref[0, h + j, :, 0:MLA_NOPE] = k2[:, j * MLA_NOPE:(j + 1) * MLA_NOPE].astype(BF16)
            kmla_ref[0, h + j, :, MLA_NOPE:MLA_QK_PAD] = kr
            vtm_ref[0, h + j, 0:MLA_V, :] = v2[j * MLA_V:(j + 1) * MLA_V, :].astype(BF16)
            vtm_ref[0, h + j, MLA_V:VT_ROWS, :] = ones

    cq_t = _rms(cq_ref[...].astype(F32), qw_ref[...]).T.astype(BF16)
    cos_t, sin_t = cmt_ref[...], smt_ref[...]
    q16 = MLA_ROPE // 4
    zero_rows = jnp.zeros((MLA_QK_PAD - MLA_NOPE - MLA_ROPE, tm), BF16)
    for h in range(MLA_HEADS):
        q = _dot(qupt_ref[h * MLA_QK_PAD:(h + 1) * MLA_QK_PAD, :], cq_t)
        x = q[MLA_NOPE:MLA_NOPE + MLA_ROPE]
        partner = jnp.concatenate([x[q16:2 * q16], x[0:q16], x[3 * q16:4 * q16], x[2 * q16:3 * q16]], axis=0)
        q_rope = x * cos_t + partner * sin_t
        qtm_ref[0, h, 0:MLA_NOPE, :] = (q[0:MLA_NOPE] * mla_scale).astype(BF16)
        qtm_ref[0, h, MLA_NOPE:MLA_NOPE + MLA_ROPE, :] = (q_rope * mla_scale).astype(BF16)
        qtm_ref[0, h, MLA_NOPE + MLA_ROPE:MLA_QK_PAD, :] = zero_rows

    for n in range(GQA_KV):
        k = _rms(za_ref[:, C_GK + n * GQA_HD:C_GK + (n + 1) * GQA_HD].astype(F32), gkw_ref[...])
        kg_ref[0, n] = _rope(k, cos_g, sin_g, GQA_HD // 4).astype(BF16)
        v = za_ref[:, C_GV + n * GQA_HD:C_GV + (n + 1) * GQA_HD]
        vtg_ref[0, n, 0:GQA_HD, :] = v.astype(F32).T.astype(BF16)
        vtg_ref[0, n, GQA_HD:VT_ROWS, :] = ones
    for h in range(GQA_HEADS):
        q = _rms(gq_ref[:, h * GQA_HD:(h + 1) * GQA_HD].astype(F32), gqw_ref[...])
        q = _rope(q, cos_g, sin_g, GQA_HD // 4) * gqa_scale
        qtg_ref[0, h] = q.T.astype(BF16)


def _prep(z, lp, tabs, *, s_tot, batch):
    tm = PREP_TILE
    tpb = s_tot // tm
    cos_m, sin_m, cos_g, sin_g = tabs
    cos_mt, sin_mt = cos_m[:, :MLA_ROPE].T, sin_m[:, :MLA_ROPE].T
    tab = lambda b, i: (i, 0)
    tab_t = lambda b, i: (0, i)
    const = lambda b, i: (0, 0)
    rows_out = lambda b, i: (b, 0, i, 0)
    cols_out = lambda b, i: (b, 0, 0, i)
    outs = [
        jax.ShapeDtypeStruct((batch, MLA_HEADS, s_tot, MLA_QK_PAD), BF16),
        jax.ShapeDtypeStruct((batch, MLA_HEADS, VT_ROWS, s_tot), BF16),
        jax.ShapeDtypeStruct((batch, MLA_HEADS, MLA_QK_PAD, s_tot), BF16),
        jax.ShapeDtypeStruct((batch, GQA_KV, s_tot, GQA_HD), BF16),
        jax.ShapeDtypeStruct((batch, GQA_KV, VT_ROWS, s_tot), BF16),
        jax.ShapeDtypeStruct((batch, GQA_HEADS, GQA_HD, s_tot), BF16),
    ]
    return pl.pallas_call(
        _prep_kernel,
        grid=(batch, tpb),
        in_specs=[
            pl.BlockSpec((tm, 2048), lambda b, i: (b * tpb + i, 0)),
            pl.BlockSpec((tm, MLA_RANK), lambda b, i: (b * tpb + i, C_CQ // MLA_RANK)),
            pl.BlockSpec((tm, 2048), lambda b, i: (b * tpb + i, C_GQ // 2048)),
            pl.BlockSpec((1, MLA_RANK), const),
            pl.BlockSpec((1, MLA_RANK), const),
            pl.BlockSpec((MLA_RANK, MLA_HEADS * MLA_NOPE), const),
            pl.BlockSpec((MLA_HEADS * MLA_V, MLA_RANK), const),
            pl.BlockSpec((MLA_HEADS * MLA_QK_PAD, MLA_RANK), const),
            pl.BlockSpec((1, GQA_HD), const),
            pl.BlockSpec((1, GQA_HD), const),
            pl.BlockSpec((tm, LANE), tab),
            pl.BlockSpec((tm, LANE), tab),
            pl.BlockSpec((MLA_ROPE, tm), tab_t),
            pl.BlockSpec((MLA_ROPE, tm), tab_t),
            pl.BlockSpec((tm, LANE), tab),
            pl.BlockSpec((tm, LANE), tab),
        ],
        out_specs=[
            pl.BlockSpec((1, MLA_HEADS, tm, MLA_QK_PAD), rows_out),
            pl.BlockSpec((1, MLA_HEADS, VT_ROWS, tm), cols_out),
            pl.BlockSpec((1, MLA_HEADS, MLA_QK_PAD, tm), cols_out),
            pl.BlockSpec((1, GQA_KV, tm, GQA_HD), rows_out),
            pl.BlockSpec((1, GQA_KV, VT_ROWS, tm), cols_out),
            pl.BlockSpec((1, GQA_HEADS, GQA_HD, tm), cols_out),
        ],
        out_shape=outs,
        compiler_params=_params(("arbitrary", "arbitrary")),
        name="attn_prep",
    )(z, z, z, lp["mla_kv_norm_w"], lp["mla_q_norm_w"], lp["mla_k_up"], lp["mla_v_up_t"], lp["mla_q_up_t"],
      lp["gqa_k_norm_w"], lp["gqa_q_norm_w"], cos_m, sin_m, cos_mt, sin_mt, cos_g, sin_g)


def _softmax_strip(s, m_old):
    m_new = jnp.maximum(m_old, jnp.max(s, axis=0, keepdims=True).astype(F32))
    alpha = jnp.exp2(m_old - m_new)
    p = jnp.exp2(s - m_new.astype(BF16))
    return m_new, alpha, p


def _scores(k, q):
    return _dot(k, q).astype(BF16)


def _attn_kernel(qt_ref, k_ref, vt_ref, o_ref, acc_scr, m_scr, s_scr, *, heads, tq, ch, n_chunks):
    n_strips = heads * tq // Q_STRIP

    def q_strip(g):
        h, off = divmod(g * Q_STRIP, tq)
        return qt_ref[0, h, :, off:off + Q_STRIP]

    def keys(c):
        return k_ref[0, 0, pl.ds(pl.multiple_of(c * ch, ch), ch), :]

    m_scr[...] = jnp.full(m_scr.shape, -1e30, F32)
    acc_scr[...] = jnp.zeros(acc_scr.shape, F32)

    def chunk(c, issue_next):
        k = keys(c)
        vt = vt_ref[0, 0, :, pl.ds(pl.multiple_of(c * ch, ch), ch)]
        m_all = m_scr[...]
        scores = [s_scr[...]] + [None] * (n_strips - 1)
        probs = [None] * n_strips
        for t in range(n_strips + 1):
            if t + 1 < n_strips:
                scores[t + 1] = _scores(k, q_strip(t + 1))
            elif t + 1 == n_strips and issue_next:
                s_scr[...] = _scores(keys(c + 1), q_strip(0))
            if t < n_strips:
                cols = slice(t * Q_STRIP, (t + 1) * Q_STRIP)
                m_new, alpha, p = _softmax_strip(scores[t], m_all[:, cols])
                probs[t] = (m_new, alpha, p)
            if t >= 1:
                cols = slice((t - 1) * Q_STRIP, t * Q_STRIP)
                m_new, alpha, p = probs[t - 1]
                m_scr[:, cols] = m_new
                acc_scr[:, cols] = alpha * acc_scr[:, cols] + _dot(vt, p)

    s_scr[...] = _scores(keys(0), q_strip(0))

    def body(c, carry):
        chunk(c, True)
        return carry

    lax.fori_loop(0, n_chunks - 1, body, 0)
    chunk(n_chunks - 1, False)

    inv_l = 1.0 / acc_scr[GQA_HD:GQA_HD + 1, :]
    for g in range(heads):
        for j in range(tq // Q_STRIP):
            cols = slice(g * tq + j * Q_STRIP, g * tq + (j + 1) * Q_STRIP)
            o = acc_scr[0:GQA_HD, cols] * inv_l[:, cols]
            o_ref[0, j * Q_STRIP:(j + 1) * Q_STRIP, g * GQA_HD:(g + 1) * GQA_HD] = o.T.astype(o_ref.dtype)


def _ctx_attn_kernel(qt_ref, k_ref, vt_ref, o_in_ref, o_ref, *, heads):
    del o_in_ref
    q = jnp.concatenate([qt_ref[0, g] for g in range(heads)], axis=1) if heads > 1 else qt_ref[0, 0]
    s = _scores(k_ref[0, 0], q)
    _, _, p = _softmax_strip(s, jnp.full((1, s.shape[1]), -1e30, F32))
    acc = _dot(vt_ref[0, 0], p)
    o = acc[0:GQA_HD] / acc[GQA_HD:GQA_HD + 1]
    n = o.shape[1] // heads
    for g in range(heads):
        o_ref[0, :, g * GQA_HD:(g + 1) * GQA_HD] = o[:, g * n:(g + 1) * n].T.astype(o_ref.dtype)


def _attention(qt, k, vt, *, heads, tq, s_lat, s_tot, name):
    batch, n_heads, dk, _ = qt.shape
    n_kv = k.shape[1]
    n_ctx = s_tot - s_lat
    tq = min(tq, s_lat)
    ch = _first_divisor(KV_CHUNKS, s_tot)
    assert s_lat % tq == 0 and tq % Q_STRIP == 0 and s_lat % n_ctx == 0
    kern = functools.partial(_attn_kernel, heads=heads, tq=tq, ch=ch, n_chunks=s_tot // ch)
    out_shape = jax.ShapeDtypeStruct((batch, s_tot, n_heads * GQA_HD), BF16)
    o = pl.pallas_call(
        kern,
        grid=(batch, n_kv, s_lat // tq),
        in_specs=[
            pl.BlockSpec((1, heads, dk, tq), lambda b, n, q: (b, n, 0, q)),
            pl.BlockSpec((1, 1, s_tot, dk), lambda b, n, q: (b, n, 0, 0)),
            pl.BlockSpec((1, 1, VT_ROWS, s_tot), lambda b, n, q: (b, n, 0, 0)),
        ],
        out_specs=pl.BlockSpec((1, tq, heads * GQA_HD), lambda b, n, q: (b, q, n)),
        out_shape=out_shape,
        scratch_shapes=[
            pltpu.VMEM((VT_ROWS, heads * tq), F32),
            pltpu.VMEM((1, heads * tq), F32),
            pltpu.VMEM((ch, Q_STRIP), BF16),
        ],
        compiler_params=_params(("arbitrary", "arbitrary", "arbitrary")),
        name=name,
    )(qt, k, vt)
    ctx_blk = s_lat // n_ctx
    return pl.pallas_call(
        functools.partial(_ctx_attn_kernel, heads=heads),
        grid=(batch, n_kv),
        in_specs=[
            pl.BlockSpec((1, heads, dk, n_ctx), lambda b, n: (b, n, 0, ctx_blk)),
            pl.BlockSpec((1, 1, n_ctx, dk), lambda b, n: (b, n, ctx_blk, 0)),
            pl.BlockSpec((1, 1, VT_ROWS, n_ctx), lambda b, n: (b, n, 0, ctx_blk)),
            pl.BlockSpec(memory_space=pl.ANY),
        ],
        out_specs=pl.BlockSpec((1, n_ctx, heads * GQA_HD), lambda b, n: (b, ctx_blk, n)),
        out_shape=out_shape,
        input_output_aliases={3: 0},
        compiler_params=_params(("arbitrary", "arbitrary")),
        name=name + "_ctx",
    )(qt, k, vt, o)


def _shift_rows(x, s, fill, reverse):
    n = x.shape[0]
    if s % SUBLANES == 0:
        pad = jnp.full((s, x.shape[1]), fill, x.dtype)
        return jnp.concatenate([x[s:], pad] if reverse else [pad, x[:n - s]], axis=0)
    t = lax.broadcasted_iota(jnp.int32, x.shape, 0)
    if reverse:
        return jnp.where(t < n - s, pltpu.roll(x, n - s, axis=0), fill)
    return jnp.where(t >= s, pltpu.roll(x, s, axis=0), fill)


def _tile_scan(a, b, h0, reverse):
    s = 1
    while s < a.shape[0]:
        b = a * _shift_rows(b, s, 0.0, reverse) + b
        a = a * _shift_rows(a, s, 1.0, reverse)
        s *= 2
    return a * h0 + b


def _gelu_tanh(x):
    return x * jax.nn.sigmoid(2.0 * math.sqrt(2.0 / math.pi) * (x + 0.044715 * x * x * x))


def _rglru_kernel(rx_ref, ry_ref, cw_ref, cb_ref, wa_ref, ba_ref, wx_ref, bx_ref, lam_ref,
                  o_ref, x_scr, hf_scr, *, s_lat, n_ctx):
    tt = SCAN_TILE
    x_scr[...] = rx_ref[...].astype(F32)

    cw = cw_ref[...]
    cb = cb_ref[...]
    lam = -lam_ref[...]
    softplus = jnp.log1p(jnp.exp(-jnp.abs(lam))) + jnp.maximum(lam, 0.0)
    zero_rows = jnp.zeros((SUBLANES, RNN_BW), F32)

    def conv_tile(row0, n, prev_tail, next_head):
        cur = x_scr[pl.ds(row0, n), :]
        ext = jnp.concatenate([prev_tail, cur, next_head], axis=0)
        xc = cb
        for j in range(CONV_W):
            o = SUBLANES - CONV_W // 2 + j
            xc = xc + ext[o:o + n, :] * cw[j:j + 1, :]
        x_scr[pl.ds(row0, n), :] = xc
        return xc, cur[n - SUBLANES:n, :]

    def coeffs(xc, d):
        xb = xc.astype(BF16)
        r = jax.nn.sigmoid(_dot(xb, wa_ref[d, 0]) + ba_ref[d:d + 1, :])
        i = jax.nn.sigmoid(_dot(xb, wx_ref[d, 0]) + bx_ref[d:d + 1, :])
        log_a = -LRU_C * r * softplus[d:d + 1, :]
        a = jnp.exp(log_a)
        y = jnp.maximum(jnp.tanh(-log_a) * (1.0 + a * a), 0.0)
        mult = jnp.where(y > 0.0, y * lax.rsqrt(y), 0.0)
        return a, mult * i * xc

    def fwd_tile(row0, n, xc, h0):
        a, b = coeffs(xc, 0)
        h = _tile_scan(a, b, h0, reverse=False)
        hf_scr[pl.ds(row0, n), :] = h
        return h[n - 1:n, :]

    def bwd_tile(row0, n, h0):
        a, b = coeffs(x_scr[pl.ds(row0, n), :], 1)
        h = _tile_scan(a, b, h0, reverse=True)
        y = ry_ref[pl.ds(row0, n), :].astype(F32)
        o_ref[pl.ds(row0, n), :] = ((hf_scr[pl.ds(row0, n), :] + h) * _gelu_tanh(y)).astype(o_ref.dtype)
        return h[0:1, :]

    n_lat_tiles = s_lat // tt
    h0 = jnp.zeros((1, RNN_BW), F32)

    xc, _ = conv_tile(s_lat, n_ctx, zero_rows, zero_rows)
    hc = fwd_tile(s_lat, n_ctx, xc, h0)

    def fwd(t, carry):
        h, tail = carry
        r0 = pl.multiple_of(t * tt, tt)
        nxt = jnp.where(t == n_lat_tiles - 1, 0.0, x_scr[pl.ds(r0 + tt, SUBLANES), :])
        xc, tail = conv_tile(r0, tt, tail, nxt)
        return fwd_tile(r0, tt, xc, h), tail

    lax.fori_loop(0, n_lat_tiles, fwd, (hc, zero_rows))

    hc = bwd_tile(s_lat, n_ctx, h0)

    def bwd(t, h):
        r0 = pl.multiple_of((n_lat_tiles - 1 - t) * tt, tt)
        return bwd_tile(r0, tt, h)

    lax.fori_loop(0, n_lat_tiles, bwd, hc)


def _rglru(z, lp, *, s_lat, s_tot, batch):
    r = z.shape[0]
    n_ctx = s_tot - s_lat
    bw = RNN_BW
    kern = functools.partial(_rglru_kernel, s_lat=s_lat, n_ctx=n_ctx)
    vec = lambda b, k: (0, k)
    return pl.pallas_call(
        kern,
        grid=(batch, RNN_BLOCKS),
        in_specs=[
            pl.BlockSpec((s_tot, bw), lambda b, k: (b, C_RX // bw + k)),
            pl.BlockSpec((s_tot, bw), lambda b, k: (b, C_RY // bw + k)),
            pl.BlockSpec((CONV_W, bw), vec),
            pl.BlockSpec((1, bw), vec),
            pl.BlockSpec((2, 1, bw, bw), lambda b, k: (0, k, 0, 0)),
            pl.BlockSpec((2, bw), vec),
            pl.BlockSpec((2, 1, bw, bw), lambda b, k: (0, k, 0, 0)),
            pl.BlockSpec((2, bw), vec),
            pl.BlockSpec((2, bw), vec),
        ],
        out_specs=pl.BlockSpec((s_tot, bw), lambda b, k: (b, k)),
        out_shape=jax.ShapeDtypeStruct((r, D_MODEL), BF16),
        scratch_shapes=[
            pltpu.VMEM((s_tot, bw), F32),
            pltpu.VMEM((s_tot, bw), F32),
        ],
        compiler_params=_params(("arbitrary", "arbitrary")),
        name="rglru",
    )(z, z, lp["rg_conv_w"], lp["rg_conv_b"], lp["rg_a_w"], lp["rg_a_b"], lp["rg_x_w"], lp["rg_x_b"],
      lp["rg_lambda"])


def _merge_kernel(om_ref, og_ref, rg_ref, gm_ref, gg_ref, gr_ref, wm_ref, wg_ref, wr_ref, y_ref):
    y = (jax.nn.sigmoid(gm_ref[...].astype(F32)) * _dot(om_ref[...], wm_ref[...])
         + jax.nn.sigmoid(gg_ref[...].astype(F32)) * _dot(og_ref[...], wg_ref[...])
         + jax.nn.sigmoid(gr_ref[...].astype(F32)) * _dot(rg_ref[...], wr_ref[...]))
    y_ref[...] = y.astype(y_ref.dtype)


def _merge(o_mla, o_gqa, rg, z, lp):
    r, d = o_mla.shape
    tm, tn = ROW_TILE, 512
    gate0 = C_GATE // tn
    per = d // tn
    act = pl.BlockSpec((tm, d), lambda i, j: (i, 0))
    wgt = pl.BlockSpec((d, tn), lambda i, j: (0, j))
    return pl.pallas_call(
        _merge_kernel,
        grid=(r // tm, d // tn),
        in_specs=[
            act, act, act,
            pl.BlockSpec((tm, tn), lambda i, j: (i, gate0 + j)),
            pl.BlockSpec((tm, tn), lambda i, j: (i, gate0 + per + j)),
            pl.BlockSpec((tm, tn), lambda i, j: (i, gate0 + 2 * per + j)),
            wgt, wgt, wgt,
        ],
        out_specs=pl.BlockSpec((tm, tn), lambda i, j: (i, j)),
        out_shape=jax.ShapeDtypeStruct((r, d), BF16),
        compiler_params=_params(("arbitrary", "arbitrary")),
        name="merge_branches",
    )(o_mla, o_gqa, rg, z, z, z, lp["mla_out"], lp["gqa_out"], lp["rg_out"])


def _resid_kernel(y_ref, w_ref, h_ref, ml_ref, mc_ref, o_ref, *, tm, tpb, s_lat, tn):
    i, j = pl.program_id(0), pl.program_id(1)
    is_ctx = _is_ctx_rows(i, tpb, tm, s_lat)
    col = pl.ds(pl.multiple_of(j * tn, tn), tn)
    gate = jnp.where(is_ctx, mc_ref[0, 2:3, col], ml_ref[0, 2:3, col])
    o_ref[...] = h_ref[...] + gate * _dot(y_ref[...], w_ref[...])


def _merge_out(y, w, h, mods, *, s_lat, s_tot):
    r, d = h.shape
    tm, tn = ROW_TILE, 1024
    tpb = s_tot // tm
    return pl.pallas_call(
        functools.partial(_resid_kernel, tm=tm, tpb=tpb, s_lat=s_lat, tn=tn),
        grid=(r // tm, d // tn),
        in_specs=[
            pl.BlockSpec((tm, d), lambda i, j: (i, 0)),
            pl.BlockSpec((d, tn), lambda i, j: (0, j)),
            pl.BlockSpec((tm, tn), lambda i, j: (i, j)),
            pl.BlockSpec((1, 6, d), lambda i, j: (i // tpb, 0, 0)),
            pl.BlockSpec((1, 6, d), lambda i, j: (2, 0, 0)),
        ],
        out_specs=pl.BlockSpec((tm, tn), lambda i, j: (i, j)),
        out_shape=jax.ShapeDtypeStruct((r, d), F32),
        compiler_params=_params(("arbitrary", "arbitrary")),
        name="merge_out_residual",
    )(y, w, h, mods, mods)


def _ffn_kernel(h_ref, nw_ref, ml_ref, mc_ref, wg_ref, wu_ref, wd_ref, fw_ref, o_ref, u_scr,
                *, tm, tpb, s_lat, final_norm):
    i, j = pl.program_id(0), pl.program_id(1)
    o_ref = o_ref.at[0]

    @pl.when(j == 0)
    def _():
        _mod_norm_to(u_scr, h_ref, nw_ref, ml_ref, mc_ref, i, tpb, tm, s_lat, 3, 4)

    u = u_scr[...]
    g = _dot(u, wg_ref[...])
    g = g * jax.nn.sigmoid(g) * _dot(u, wu_ref[...])
    part = _dot(g.astype(BF16), wd_ref[...])

    @pl.when(j == 0)
    def _():
        o_ref[...] = part

    @pl.when(j > 0)
    def _():
        o_ref[...] += part

    @pl.when(j == pl.num_programs(1) - 1)
    def _():
        fw = fw_ref[...]
        for r0 in range(0, tm, ROW_CHUNK):
            rows = slice(r0, r0 + ROW_CHUNK)
            is_ctx = _is_ctx_rows(i, tpb, tm, s_lat, r0, ROW_CHUNK)
            gate = jnp.where(is_ctx, mc_ref[0, 5:6], ml_ref[0, 5:6])
            out = h_ref[rows, :] + gate * o_ref[rows, :]
            if final_norm:
                out = _rms(out, fw)
            o_ref[rows, :] = out


def _ffn(h, nw, mods, wg, wu, wd, fw, *, s_lat, s_tot, final_norm):
    r, d = h.shape
    out_rows = s_lat if final_norm else s_tot
    hid = wg.shape[1]
    tm, th = ROW_TILE, 512
    tpb = s_tot // tm
    kern = functools.partial(_ffn_kernel, tm=tm, tpb=tpb, s_lat=s_lat, final_norm=final_norm)
    return pl.pallas_call(
        kern,
        grid=(r // tm, hid // th),
        in_specs=[
            pl.BlockSpec((tm, d), lambda i, j: (i, 0)),
            pl.BlockSpec((1, d), lambda i, j: (0, 0)),
            pl.BlockSpec((1, 6, d), lambda i, j: (i // tpb, 0, 0)),
            pl.BlockSpec((1, 6, d), lambda i, j: (2, 0, 0)),
            pl.BlockSpec((d, th), lambda i, j: (0, j)),
            pl.BlockSpec((d, th), lambda i, j: (0, j)),
            pl.BlockSpec((th, d), lambda i, j: (j, 0)),
            pl.BlockSpec((1, d), lambda i, j: (0, 0)),
        ],
        out_specs=pl.BlockSpec((1, tm, d), lambda i, j: (i // tpb, i % tpb, 0)),
        out_shape=jax.ShapeDtypeStruct((r // s_tot, out_rows, d), F32),
        scratch_shapes=[pltpu.VMEM((tm, d), BF16)],
        compiler_params=_params(("arbitrary", "arbitrary")),
        name="ffn",
    )(h, nw, mods, mods, wg, wu, wd, fw)


def _rope_tables(s_lat, n_ctx, dim):
    quarter = dim // 4
    inv = ROPE_THETA ** (-jnp.arange(quarter, dtype=F32) / quarter)
    rows = s_lat // GRID_W
    ang_r = jnp.arange(rows, dtype=F32)[:, None] * inv
    ang_c = jnp.arange(GRID_W, dtype=F32)[:, None] * inv
    by_row = lambda v: jnp.repeat(v, GRID_W, axis=0)
    by_col = lambda v: jnp.tile(v, (rows, 1))
    cos_r, sin_r, cos_c, sin_c = by_row(jnp.cos(ang_r)), by_row(jnp.sin(ang_r)), by_col(jnp.cos(ang_c)), by_col(jnp.sin(ang_c))
    cos = jnp.concatenate([cos_r, cos_r, cos_c, cos_c], axis=1)
    sin = jnp.concatenate([-sin_r, sin_r, -sin_c, sin_c], axis=1)
    cos = jnp.concatenate([cos, jnp.ones((n_ctx, dim), F32)], axis=0)
    sin = jnp.concatenate([sin, jnp.zeros((n_ctx, dim), F32)], axis=0)
    padw = LANE - dim
    if padw:
        cos = jnp.pad(cos, ((0, 0), (0, padw)))
        sin = jnp.pad(sin, ((0, 0), (0, padw)))
    return cos, sin


def _pad_w_in(w):
    d = w.shape[0]
    z = lambda n: jnp.zeros((d, n), w.dtype)
    ckv_kr = w[:, 0:576]
    gk_gv = w[:, 576:1600]
    rx = w[:, 1600:3648]
    cq = w[:, 3648:4160]
    gq = w[:, 4160:6208]
    ry = w[:, 6208:8256]
    gate = w[:, 8256:14400]
    return jnp.concatenate([ckv_kr, z(64), gk_gv, z(384), rx, gq, ry, gate, cq], axis=1).astype(BF16)


def _split_kv_up(w):
    rank = w.shape[0]
    w = w.reshape(rank, MLA_HEADS, MLA_NOPE + MLA_V).astype(BF16)
    k = w[:, :, :MLA_NOPE].reshape(rank, MLA_HEADS * MLA_NOPE)
    v = w[:, :, MLA_NOPE:].reshape(rank, MLA_HEADS * MLA_V)
    return k, v.T


def _pad_q_up(w):
    rank = w.shape[0]
    w = w.reshape(rank, MLA_HEADS, MLA_NOPE + MLA_ROPE)
    w = jnp.pad(w, ((0, 0), (0, 0), (0, MLA_QK_PAD - MLA_NOPE - MLA_ROPE)))
    return w.reshape(rank, MLA_HEADS * MLA_QK_PAD).astype(BF16)


def kernel(x, c, ctx, c_ctx, mod_w, mod_b, norm_mix_w, norm_ffn_w, w_in, mla_q_norm_w, mla_q_up,
           mla_kv_norm_w, mla_kv_up, mla_out, gqa_q_norm_w, gqa_k_norm_w, gqa_out, rg_conv_w, rg_conv_b,
           rg_a_w, rg_a_b, rg_x_w, rg_x_b, rg_lambda, rg_out, merge_out, ffn_w_gate, ffn_w_up, ffn_w_down,
           final_norm_w):
    batch, s_lat, d = x.shape
    n_ctx = ctx.shape[1]
    depth = mod_w.shape[0]
    s_tot = s_lat + n_ctx
    assert batch == 2 and d == D_MODEL and n_ctx == SCAN_TILE
    assert s_tot % ROW_TILE == 0 and s_tot % PREP_TILE == 0 and s_lat % SCAN_TILE == 0 and s_lat % GRID_W == 0

    h = jnp.concatenate([x, ctx], axis=1).reshape(batch * s_tot, d)
    cs = jnp.zeros((8, d), F32).at[0:batch].set(c).at[batch].set(c_ctx)
    mods_all = _mods(cs, mod_w, mod_b).reshape(depth, 8, 6, d)
    tabs = _rope_tables(s_lat, n_ctx, MLA_ROPE) + _rope_tables(s_lat, n_ctx, GQA_HD)
    row2 = lambda v: v.reshape(1, -1)
    dims = dict(s_lat=s_lat, s_tot=s_tot)

    for l in range(depth):
        last = l == depth - 1
        mods = mods_all[l]
        lp = {
            "mla_kv_norm_w": row2(mla_kv_norm_w[l]), "mla_q_norm_w": row2(mla_q_norm_w[l]),
            "mla_k_up": _split_kv_up(mla_kv_up[l])[0], "mla_v_up_t": _split_kv_up(mla_kv_up[l])[1],
            "mla_q_up_t": _pad_q_up(mla_q_up[l]).T,
            "gqa_k_norm_w": row2(gqa_k_norm_w[l]), "gqa_q_norm_w": row2(gqa_q_norm_w[l]),
            "rg_conv_w": rg_conv_w[l], "rg_conv_b": row2(rg_conv_b[l]),
            "rg_a_w": rg_a_w[l].astype(BF16), "rg_a_b": rg_a_b[l],
            "rg_x_w": rg_x_w[l].astype(BF16), "rg_x_b": rg_x_b[l], "rg_lambda": rg_lambda[l],
            "mla_out": mla_out[l].astype(BF16), "gqa_out": gqa_out[l].astype(BF16),
            "rg_out": rg_out[l].astype(BF16),
        }
        z = _inproj(h, row2(norm_mix_w[l]), mods, _pad_w_in(w_in[l]), **dims)
        kmla, vtm, qtm, kg, vtg, qtg = _prep(z, lp, tabs, s_tot=s_tot, batch=batch)
        o_mla = _attention(qtm, kmla, vtm, heads=1, tq=MLA_Q_TILE, name="mla_attention", **dims)
        o_gqa = _attention(qtg, kg, vtg, heads=GQA_GROUP, tq=GQA_Q_TILE, name="gqa_attention", **dims)
        o_mla = o_mla.reshape(batch * s_tot, d)
        o_gqa = o_gqa.reshape(batch * s_tot, d)
        rg = _rglru(z, lp, batch=batch, **dims)
        y = _merge(o_mla, o_gqa, rg, z, lp)
        h = _merge_out(y, merge_out[l].astype(BF16), h, mods, **dims)
        h = _ffn(h, row2(norm_ffn_w[l]), mods, ffn_w_gate[l].astype(BF16), ffn_w_up[l].astype(BF16),
                 ffn_w_down[l].astype(BF16), row2(final_norm_w), final_norm=last, **dims)
        if not last:
            h = h.reshape(batch * s_tot, d)

    return h
```
